```python
import jax, jax.numpy as jnp
from jax import lax
import numpy as np

D_MODEL = 2048
BATCH = 4
SEQ = 8192
DEPTH = 1

CHUNK = 64
D_HGRN = D_MODEL // 2
HGRN_HEAD_DIM = 128
HGRN_HEADS = D_HGRN // HGRN_HEAD_DIM
D_POOL = D_MODEL - D_HGRN
POOL_WINDOWS = (2, 4, 8, 16)
N_POOL_GROUPS = len(POOL_WINDOWS)
POOL_GROUP_DIM = D_POOL // N_POOL_GROUPS
D_IN = 4 * D_HGRN + D_POOL
D_FF = 5632
RMS_EPS = 1e-6

kernel_name = "hymba_style_hgrn2_pool_macaron_block"


def rmsnorm(x, gain):
    x32 = x.astype(jnp.float32)
    y = x32 * lax.rsqrt(jnp.mean(x32 * x32, axis=-1, keepdims=True) + RMS_EPS)
    return (y * gain.astype(jnp.float32)).astype(x.dtype)


def swiglu(h, w_gate, w_up, w_down):
    return (jax.nn.silu(h @ w_gate) * (h @ w_up)) @ w_down


def hgrn2_mixer(q, f_logit, i, g, lb, g_norm):
    b_sz, t_len, _ = q.shape
    n_chunks = t_len // CHUNK
    f = lb.astype(jnp.float32) + (1.0 - lb.astype(jnp.float32)) * jax.nn.sigmoid(f_logit.astype(jnp.float32))
    log_f = jnp.log(f)
    k = 1.0 - f

    def to_chunks(a):
        a = a.astype(jnp.float32).reshape(b_sz, n_chunks, CHUNK, HGRN_HEADS, HGRN_HEAD_DIM)
        return a.transpose(1, 0, 3, 2, 4)

    qs, ks, vs, gs = to_chunks(q), to_chunks(k), to_chunks(i), to_chunks(log_f)
    causal = jnp.tril(jnp.ones((CHUNK, CHUNK), dtype=bool))

    def step(state, xs):
        qc, kc, vc, gc = xs
        cum = jnp.cumsum(gc, axis=2)
        diff = cum[:, :, :, None, :] - cum[:, :, None, :, :]
        decay = jnp.exp(jnp.where(causal[None, None, :, :, None], diff, -jnp.inf))
        scores = jnp.einsum('bhtd,bhsd,bhtsd->bhts', qc, kc, decay)
        o_intra = jnp.einsum('bhts,bhsv->bhtv', scores, vc)
        o_inter = jnp.einsum('bhtk,bhkv->bhtv', qc * jnp.exp(cum), state)
        last = cum[:, :, -1:, :]
        new_state = jnp.exp(last[:, :, 0, :])[..., None] * state + jnp.einsum(
            'bhsk,bhsv->bhkv', kc * jnp.exp(last - cum), vc)
        return new_state, o_intra + o_inter

    state0 = jnp.zeros((b_sz, HGRN_HEADS, HGRN_HEAD_DIM, HGRN_HEAD_DIM), jnp.float32)
    _, o = lax.scan(step, state0, (qs, ks, vs, gs))
    o = o.transpose(1, 0, 3, 2, 4).reshape(b_sz, t_len, HGRN_HEADS, HGRN_HEAD_DIM)
    o = rmsnorm(o, g_norm).reshape(b_sz, t_len, D_HGRN)
    return (o * jax.nn.silu(g.astype(jnp.float32))).astype(q.dtype)


def pool_mixer(u, w_pool, pool_scale):
    t_len = u.shape[1]
    u32 = u.astype(jnp.float32)
    pos = jnp.arange(t_len, dtype=jnp.int32)
    outs = []
    for gi, w in enumerate(POOL_WINDOWS):
        seg = u32[..., gi * POOL_GROUP_DIM:(gi + 1) * POOL_GROUP_DIM]
        csum = jnp.cumsum(seg, axis=1)
        csum_prev = jnp.pad(csum, ((0, 0), (w, 0), (0, 0)))[:, :t_len]
        count = jnp.minimum(pos + 1, w).astype(jnp.float32)[None, :, None]
        pooled = (csum - csum_prev) / count - seg
        outs.append(jnp.einsum('btc,cd->btd', pooled, w_pool[gi].astype(jnp.float32)))
    y = jnp.concatenate(outs, axis=-1) * pool_scale.astype(jnp.float32)
    return y.astype(u.dtype)


def setup_inputs(seed: int = 0) -> dict:
    key = jax.random.key(seed)
    ks = jax.random.split(key, 20)
    n = jax.random.normal

    def gain(k, d):
        return 1.0 + 0.02 * n(k, (DEPTH, d), jnp.float32)

    return {
        "x": n(ks[0], (BATCH, SEQ, D_MODEL), jnp.float32),
        "ffn1_pre_g": gain(ks[1], D_MODEL),
        "ffn1_w_gate": n(ks[2], (DEPTH, D_MODEL, D_FF), jnp.float32) * D_MODEL ** -0.5,
        "ffn1_w_up": n(ks[3], (DEPTH, D_MODEL, D_FF), jnp.float32) * D_MODEL ** -0.5,
        "ffn1_w_down": n(ks[4], (DEPTH, D_FF, D_MODEL), jnp.float32) * D_FF ** -0.5,
        "ffn1_post_g": gain(ks[5], D_MODEL),
        "mix_pre_g": gain(ks[6], D_MODEL),
        "w_in": n(ks[7], (DEPTH, D_MODEL, D_IN), jnp.float32) * D_MODEL ** -0.5,
        "hgrn_lb_logits": n(ks[8], (DEPTH + 1, D_HGRN), jnp.float32),
        "hgrn_g_norm": gain(ks[9], HGRN_HEAD_DIM),
        "w_pool": n(ks[10], (DEPTH, N_POOL_GROUPS, POOL_GROUP_DIM, POOL_GROUP_DIM), jnp.float32) * POOL_GROUP_DIM ** -0.5,
        "pool_scale": gain(ks[11], D_POOL),
        "w_out": n(ks[12], (DEPTH, D_MODEL, D_MODEL), jnp.float32) * D_MODEL ** -0.5,
        "mix_post_g": gain(ks[13], D_MODEL),
        "ffn2_pre_g": gain(ks[14], D_MODEL),
        "ffn2_w_gate": n(ks[15], (DEPTH, D_MODEL, D_FF), jnp.float32) * D_MODEL ** -0.5,
        "ffn2_w_up": n(ks[16], (DEPTH, D_MODEL, D_FF), jnp.float32) * D_MODEL ** -0.5,
        "ffn2_w_down": n(ks[17], (DEPTH, D_FF, D_MODEL), jnp.float32) * D_FF ** -0.5,
        "ffn2_post_g": gain(ks[18], D_MODEL),
    }


def reference(x, ffn1_pre_g, ffn1_w_gate, ffn1_w_up, ffn1_w_down, ffn1_post_g,
              mix_pre_g, w_in, hgrn_lb_logits, hgrn_g_norm, w_pool, pool_scale,
              w_out, mix_post_g, ffn2_pre_g, ffn2_w_gate, ffn2_w_up, ffn2_w_down,
              ffn2_post_g):
    lb_all = jnp.cumsum(jax.nn.softmax(hgrn_lb_logits.astype(jnp.float32), axis=0), axis=0)
    split_at = [D_HGRN, 2 * D_HGRN, 3 * D_HGRN, 4 * D_HGRN]
    for l in range(DEPTH):
        h = swiglu(rmsnorm(x, ffn1_pre_g[l]), ffn1_w_gate[l], ffn1_w_up[l], ffn1_w_down[l])
        x = x + 0.5 * rmsnorm(h, ffn1_post_g[l])
        h = rmsnorm(x, mix_pre_g[l])
        proj = h @ w_in[l]
        q, f_logit, i, g, u = jnp.split(proj, split_at, axis=-1)
        y_rec = hgrn2_mixer(q, f_logit, i, g, lb_all[l], hgrn_g_norm[l])
        y_pool = pool_mixer(u, w_pool[l], pool_scale[l])
        y = jnp.concatenate([y_rec, y_pool], axis=-1) @ w_out[l]
        x = x + rmsnorm(y, mix_post_g[l])
        h = swiglu(rmsnorm(x, ffn2_pre_g[l]), ffn2_w_gate[l], ffn2_w_up[l], ffn2_w_down[l])
        x = x + 0.5 * rmsnorm(h, ffn2_post_g[l])
    return x
```

```python
import functools
import math

import numpy as np
import jax
import jax.numpy as jnp
from jax import lax
from jax.experimental import pallas as pl
from jax.experimental.pallas import tpu as pltpu

RMS_EPS = 1e-6
HEAD_DIM = 128
POOL_WINDOWS = (2, 4, 8, 16)
POOL_HALO = 16
V7X_VMEM_LIMIT_BYTES = 56 * 1024 * 1024

FFN_ROWS = 512
FFN_COLS = 512
INPROJ_ROWS = 512
INPROJ_COLS = 1280
HGRN_CHUNK = 128
MIXOUT_ROWS = 512

BF16 = jnp.bfloat16
F32 = jnp.float32


def _rms_scale(v):
    return lax.rsqrt(jnp.mean(v * v, axis=-1, keepdims=True) + RMS_EPS)


def _params(semantics):
    return pltpu.CompilerParams(dimension_semantics=semantics,
                                vmem_limit_bytes=V7X_VMEM_LIMIT_BYTES)


def _ffn_body(x_ref, pre_g_ref, wg_ref, wu_ref, wd_ref, post_g_ref, o_ref, hn_ref):
    j = pl.program_id(1)

    @pl.when(j == 0)
    def _():
        x = x_ref[...]
        hn_ref[...] = (x * _rms_scale(x) * pre_g_ref[...]).astype(BF16)
        o_ref[...] = jnp.zeros_like(o_ref)

    hn = hn_ref[...]
    g = jnp.dot(hn, wg_ref[...], preferred_element_type=F32)
    u = jnp.dot(hn, wu_ref[...], preferred_element_type=F32)
    a = (g * jax.nn.sigmoid(g) * u).astype(BF16)
    o_ref[...] += jnp.dot(a, wd_ref[...], preferred_element_type=F32)

    @pl.when(j == pl.num_programs(1) - 1)
    def _():
        h = o_ref[...]
        o_ref[...] = x_ref[...] + 0.5 * (h * _rms_scale(h) * post_g_ref[...])


def _ffn(x, pre_g, wg, wu, wd, post_g):
    n, d = x.shape
    dff = wg.shape[1]
    grid = (n // FFN_ROWS, dff // FFN_COLS)
    return pl.pallas_call(
        _ffn_body,
        grid=grid,
        in_specs=[
            pl.BlockSpec((FFN_ROWS, d), lambda i, j: (i, 0)),
            pl.BlockSpec((1, d), lambda i, j: (0, 0)),
            pl.BlockSpec((d, FFN_COLS), lambda i, j: (0, j)),
            pl.BlockSpec((d, FFN_COLS), lambda i, j: (0, j)),
            pl.BlockSpec((FFN_COLS, d), lambda i, j: (j, 0)),
            pl.BlockSpec((1, d), lambda i, j: (0, 0)),
        ],
        out_specs=pl.BlockSpec((FFN_ROWS, d), lambda i, j: (i, 0)),
        out_shape=jax.ShapeDtypeStruct((n, d), F32),
        scratch_shapes=[pltpu.VMEM((FFN_ROWS, d), BF16)],
        compiler_params=_params(("parallel", "arbitrary")),
        name="ffn",
    )(x, pre_g, wg, wu, wd, post_g)


def _inproj_body(x_ref, g_ref, w_ref, o_ref, hn_ref):
    @pl.when(pl.program_id(1) == 0)
    def _():
        x = x_ref[...]
        hn_ref[...] = (x * _rms_scale(x) * g_ref[...]).astype(BF16)

    o_ref[...] = jnp.dot(hn_ref[...], w_ref[...], preferred_element_type=F32)


def _inproj(x, gain, w):
    n, d = x.shape
    dout = w.shape[1]
    grid = (n // INPROJ_ROWS, dout // INPROJ_COLS)
    return pl.pallas_call(
        _inproj_body,
        grid=grid,
        in_specs=[
            pl.BlockSpec((INPROJ_ROWS, d), lambda i, j: (i, 0)),
            pl.BlockSpec((1, d), lambda i, j: (0, 0)),
            pl.BlockSpec((d, INPROJ_COLS), lambda i, j: (0, j)),
        ],
        out_specs=pl.BlockSpec((INPROJ_ROWS, INPROJ_COLS), lambda i, j: (i, j)),
        out_shape=jax.ShapeDtypeStruct((n, dout), F32),
        scratch_shapes=[pltpu.VMEM((INPROJ_ROWS, d), BF16)],
        compiler_params=_params(("parallel", "arbitrary")),
        name="inproj",
    )(x, gain, w)


def _level_masks(chunk):
    t = np.arange(chunk)[:, None]
    s = np.arange(chunk)[None, :]
    masks = []
    for lh in range(int(math.log2(chunk))):
        same = (t >> (lh + 1)) == (s >> (lh + 1))
        masks.append(same & (((t >> lh) & 1) == 1) & (((s >> lh) & 1) == 0))
    return np.stack(masks).astype(np.float32)


def _midpoint_rows(cum, h, row):
    n = cum.shape[0]
    if h == 1:
        return jnp.where((row & 1) == 1, pltpu.roll(cum, 1, 0), cum)
    if h == 2:
        pos = row & 3
        nxt = pltpu.roll(cum, n - 1, 0)
        prv = pltpu.roll(cum, 1, 0)
        prv2 = pltpu.roll(cum, 2, 0)
        return jnp.where(pos == 0, nxt, jnp.where(pos == 1, cum, jnp.where(pos == 2, prv, prv2)))
    blocks = cum.reshape(n // (2 * h), 2 * h, cum.shape[1])
    mid = jnp.broadcast_to(blocks[:, h - 1:h, :], blocks.shape)
    return mid.reshape(cum.shape)


def _hgrn_head(q, fl, v, gate, lb, gn, state_t, masks_ref, row):
    n = q.shape[0]
    f = lb + (1.0 - lb) * jax.nn.sigmoid(fl)
    logf = jnp.log(f)
    k = 1.0 - f

    cum = logf
    shift = 1
    while shift < n:
        cum = cum + jnp.where(row >= shift, pltpu.roll(cum, shift, 0), 0.0)
        shift *= 2

    scores = jnp.zeros((n, n), F32)
    for lh in range(int(math.log2(n))):
        h = 1 << lh
        decay = jnp.exp(-jnp.abs(cum - _midpoint_rows(cum, h, row)))
        part = lax.dot_general((q * decay).astype(BF16), (k * decay).astype(BF16),
                               (((1,), (1,)), ((), ())), preferred_element_type=F32)
        scores = scores + part * masks_ref[lh]

    v16 = v.astype(BF16)
    o = jnp.dot(scores.astype(BF16), v16, preferred_element_type=F32)
    o = o + jnp.sum(q * k, axis=-1, keepdims=True) * v
    o = o + lax.dot_general((q * jnp.exp(cum)).astype(BF16), state_t.astype(BF16),
                            (((1,), (1,)), ((), ())), preferred_element_type=F32)

    last = cum[n - 1:n, :]
    new_state_t = state_t * jnp.exp(last) + lax.dot_general(
        v16, (k * jnp.exp(last - cum)).astype(BF16),
        (((0,), (0,)), ((), ())), preferred_element_type=F32)

    y = o * _rms_scale(o) * gn * (gate * jax.nn.sigmoid(gate))
    return y, new_state_t


def _hgrn_body(q_ref, f_ref, i_ref, g_ref, lbl_ref, gn_ref, masks_ref, y_ref, state_ref):
    @pl.when(pl.program_id(1) == 0)
    def _():
        state_ref[...] = jnp.zeros_like(state_ref)

    logits = lbl_ref[...]
    e = jnp.exp(logits - jnp.max(logits, axis=0, keepdims=True))
    lb_all = e[0:1, :] / jnp.sum(e, axis=0, keepdims=True)
    gn = gn_ref[...]
    n = q_ref.shape[0]
    row = lax.broadcasted_iota(jnp.int32, (n, HEAD_DIM), 0)

    for hd in range(q_ref.shape[1] // HEAD_DIM):
        sl = slice(hd * HEAD_DIM, (hd + 1) * HEAD_DIM)
        y, new_state = _hgrn_head(q_ref[:, sl], f_ref[:, sl], i_ref[:, sl], g_ref[:, sl],
                                  lb_all[:, sl], gn, state_ref[hd], masks_ref, row)
        state_ref[hd] = new_state
        y_ref[:, sl] = y.astype(y_ref.dtype)


def _hgrn(proj, lb_logits, g_norm, d_hgrn):
    b, t, _ = proj.shape
    heads = d_hgrn // HEAD_DIM
    masks = jnp.asarray(_level_masks(HGRN_CHUNK))
    col = lambda kk: pl.BlockSpec((None, HGRN_CHUNK, d_hgrn), lambda bi, ci, kk=kk: (bi, ci, kk))
    return pl.pallas_call(
        _hgrn_body,
        grid=(b, t // HGRN_CHUNK),
        in_specs=[
            col(0), col(1), col(2), col(3),
            pl.BlockSpec(lb_logits.shape, lambda bi, ci: (0, 0)),
            pl.BlockSpec((1, HEAD_DIM), lambda bi, ci: (0, 0)),
            pl.BlockSpec(masks.shape, lambda bi, ci: (0, 0, 0)),
        ],
        out_specs=pl.BlockSpec((None, HGRN_CHUNK, d_hgrn), lambda bi, ci: (bi, ci, 0)),
        out_shape=jax.ShapeDtypeStruct((b, t, d_hgrn), BF16),
        scratch_shapes=[pltpu.VMEM((heads, HEAD_DIM, HEAD_DIM), F32)],
        compiler_params=_params(("arbitrary", "arbitrary")),
        name="hgrn",
    )(proj, proj, proj, proj, lb_logits, g_norm, masks)


def _mixout_body(yrec_ref, u_ref, halo_ref, wpool_ref, pscale_ref, wout_ref, post_g_ref, x_ref, o_ref):
    ti = pl.program_id(1)
    rows = u_ref.shape[0]
    d_rec = yrec_ref.shape[1]
    u = u_ref[...]
    halo = jnp.where(ti > 0, halo_ref[...], 0.0)
    ext = jnp.concatenate([halo, u], axis=0)
    pos = ti * rows + lax.broadcasted_iota(jnp.int32, (rows, 1), 0)
    group = u.shape[1] // len(POOL_WINDOWS)

    mixed = []
    for gi, w in enumerate(POOL_WINDOWS):
        seg = ext[:, gi * group:(gi + 1) * group]
        acc = seg
        span = 1
        while span < w:
            acc = acc + pltpu.roll(acc, span, 0)
            span *= 2
        count = jnp.minimum(pos + 1, w).astype(F32)
        pooled = acc[POOL_HALO:, :] / count - seg[POOL_HALO:, :]
        mixed.append(jnp.dot(pooled.astype(BF16), wpool_ref[gi], preferred_element_type=F32))
    y_pool = jnp.concatenate(mixed, axis=-1) * pscale_ref[...]

    y = jnp.dot(yrec_ref[...], wout_ref[0:d_rec, :], preferred_element_type=F32)
    y = y + jnp.dot(y_pool.astype(BF16), wout_ref[d_rec:, :], preferred_element_type=F32)
    o_ref[...] = x_ref[...] + y * _rms_scale(y) * post_g_ref[...]


def _mixout(yrec, proj, w_pool, pool_scale, w_out, post_g, x, tokens_per_seq):
    n, d = x.shape
    d_rec = yrec.shape[1]
    d_pool = pool_scale.shape[1]
    u_col = (proj.shape[1] - d_pool) // d_pool
    tiles = tokens_per_seq // MIXOUT_ROWS
    halo_per_tile = MIXOUT_ROWS // POOL_HALO
    return pl.pallas_call(
        _mixout_body,
        grid=(n // tokens_per_seq, tiles),
        in_specs=[
            pl.BlockSpec((MIXOUT_ROWS, d_rec), lambda bi, ti: (bi * tiles + ti, 0)),
            pl.BlockSpec((MIXOUT_ROWS, d_pool), lambda bi, ti: (bi * tiles + ti, u_col)),
            pl.BlockSpec((POOL_HALO, d_pool),
                         lambda bi, ti: (jnp.maximum((bi * tiles + ti) * halo_per_tile - 1, 0), u_col)),
            pl.BlockSpec(w_pool.shape, lambda bi, ti: (0, 0, 0)),
            pl.BlockSpec((1, d_pool), lambda bi, ti: (0, 0)),
            pl.BlockSpec(w_out.shape, lambda bi, ti: (0, 0)),
            pl.BlockSpec((1, d), lambda bi, ti: (0, 0)),
            pl.BlockSpec((MIXOUT_ROWS, d), lambda bi, ti: (bi * tiles + ti, 0)),
        ],
        out_specs=pl.BlockSpec((MIXOUT_ROWS, d), lambda bi, ti: (bi * tiles + ti, 0)),
        out_shape=jax.ShapeDtypeStruct((n, d), F32),
        compiler_params=_params(("parallel", "arbitrary")),
        name="mixout",
    )(yrec, proj, proj, w_pool, pool_scale, w_out, post_g, x)


def kernel(x, ffn1_pre_g, ffn1_w_gate, ffn1_w_up, ffn1_w_down, ffn1_post_g, mix_pre_g, w_in, hgrn_lb_logits, hgrn_g_norm, w_pool, pool_scale, w_out, mix_post_g, ffn2_pre_g, ffn2_w_gate, ffn2_w_up, ffn2_w_down, ffn2_post_g):
    b, t, d = x.shape
    n = b * t
    d_hgrn = hgrn_lb_logits.shape[1]
    depth = w_in.shape[0]
    assert depth == 1 and hgrn_lb_logits.shape[0] == depth + 1
    assert w_in.shape[2] == 4 * d_hgrn + pool_scale.shape[1]
    assert t % HGRN_CHUNK == 0 and t % MIXOUT_ROWS == 0 and n % FFN_ROWS == 0

    h = x.reshape(n, d)
    h = _ffn(h, ffn1_pre_g, ffn1_w_gate[0].astype(BF16), ffn1_w_up[0].astype(BF16),
             ffn1_w_down[0].astype(BF16), ffn1_post_g)
    proj = _inproj(h, mix_pre_g, w_in[0].astype(BF16))
    y_rec = _hgrn(proj.reshape(b, t, -1), hgrn_lb_logits, hgrn_g_norm, d_hgrn)
    h = _mixout(y_rec.reshape(n, d_hgrn), proj, w_pool[0].astype(BF16), pool_scale,
                w_out[0].astype(BF16), mix_post_g, h, t)
    h = _ffn(h, ffn2_pre_g, ffn2_w_gate[0].astype(BF16), ffn2_w_up[0].astype(BF16),
             ffn2_w_down[0].astype(BF16), ffn2_post_g)
    return h.reshape(b, t, d)
```

```python
import functools
import math

import numpy as np
import jax
import jax.numpy as jnp
from jax import lax
from jax.experimental import pallas as pl
from jax.experimental.pallas import tpu as pltpu

RMS_EPS = 1e-6
HEAD_DIM = 128
POOL_WINDOWS = (2, 4, 8, 16)
POOL_HALO = 16
V7X_VMEM_LIMIT_BYTES = 56 * 1024 * 1024

FFN_ROWS = 1024
FFN_COLS = 512
FFN_ROW_BLOCK = 512
INPROJ_ROWS = 512
INPROJ_ROW_BLOCK = 256
HGRN_CHUNK = 128
MIXOUT_ROWS = 512
MIXOUT_ROW_BLOCK = 256

BF16 = jnp.bfloat16
F32 = jnp.float32


def _rms_scale(v):
    return lax.rsqrt(jnp.mean(v * v, axis=-1, keepdims=True) + RMS_EPS)


def _params(semantics):
    return pltpu.CompilerParams(dimension_semantics=semantics,
                                vmem_limit_bytes=V7X_VMEM_LIMIT_BYTES)


def _resident(shape):
    return pl.BlockSpec(shape, lambda *_: (0,) * len(shape), pipeline_mode=pl.Buffered(1))


def _ffn_step(x_ref, pre_g_ref, wg_ref, wu_ref, wd_ref, post_g_ref, o_ref, hn_ref, *, first, last):
    for r in range(x_ref.shape[0] // FFN_ROW_BLOCK):
        rows = pl.ds(r * FFN_ROW_BLOCK, FFN_ROW_BLOCK)
        if first:
            x = x_ref[rows, :]
            hn = (x * _rms_scale(x) * pre_g_ref[...]).astype(BF16)
            hn_ref[rows, :] = hn
        else:
            hn = hn_ref[rows, :]
        g = jnp.dot(hn, wg_ref[...], preferred_element_type=F32)
        u = jnp.dot(hn, wu_ref[...], preferred_element_type=F32)
        a = (g * jax.nn.sigmoid(g) * u).astype(BF16)
        acc = jnp.dot(a, wd_ref[...], preferred_element_type=F32)
        if not first:
            acc = o_ref[rows, :] + acc
        if last:
            acc = x_ref[rows, :] + 0.5 * (acc * _rms_scale(acc) * post_g_ref[...])
        o_ref[rows, :] = acc


def _ffn_body(*refs):
    j = pl.program_id(1)
    last_j = pl.num_programs(1) - 1
    pl.when(j == 0)(functools.partial(_ffn_step, *refs, first=True, last=False))
    pl.when((j > 0) & (j < last_j))(functools.partial(_ffn_step, *refs, first=False, last=False))
    pl.when(j == last_j)(functools.partial(_ffn_step, *refs, first=False, last=True))


def _ffn(x, pre_g, wg, wu, wd, post_g):
    n, d = x.shape
    dff = wg.shape[1]
    grid = (n // FFN_ROWS, dff // FFN_COLS)
    return pl.pallas_call(
        _ffn_body,
        grid=grid,
        in_specs=[
            pl.BlockSpec((FFN_ROWS, d), lambda i, j: (i, 0)),
            pl.BlockSpec((1, d), lambda i, j: (0, 0)),
            pl.BlockSpec((d, FFN_COLS), lambda i, j: (0, j)),
            pl.BlockSpec((d, FFN_COLS), lambda i, j: (0, j)),
            pl.BlockSpec((FFN_COLS, d), lambda i, j: (j, 0)),
            pl.BlockSpec((1, d), lambda i, j: (0, 0)),
        ],
        out_specs=pl.BlockSpec((FFN_ROWS, d), lambda i, j: (i, 0)),
        out_shape=jax.ShapeDtypeStruct((n, d), F32),
        scratch_shapes=[pltpu.VMEM((FFN_ROWS, d), BF16)],
        compiler_params=_params(("parallel", "arbitrary")),
        name="ffn",
    )(x, pre_g, wg, wu, wd, post_g)


def _inproj_body(x_ref, g_ref, w_ref, o_ref):
    for r in range(x_ref.shape[0] // INPROJ_ROW_BLOCK):
        rows = pl.ds(r * INPROJ_ROW_BLOCK, INPROJ_ROW_BLOCK)
        x = x_ref[rows, :]
        hn = (x * _rms_scale(x) * g_ref[...]).astype(BF16)
        o_ref[rows, :] = jnp.dot(hn, w_ref[...], preferred_element_type=F32)


def _inproj(x, gain, w):
    n, d = x.shape
    dout = w.shape[1]
    return pl.pallas_call(
        _inproj_body,
        grid=(n // INPROJ_ROWS,),
        in_specs=[
            pl.BlockSpec((INPROJ_ROWS, d), lambda i: (i, 0)),
            _resident((1, d)),
            _resident(w.shape),
        ],
        out_specs=pl.BlockSpec((INPROJ_ROWS, dout), lambda i: (i, 0)),
        out_shape=jax.ShapeDtypeStruct((n, dout), F32),
        compiler_params=_params(("parallel",)),
        name="inproj",
    )(x, gain, w)


def _level_masks(chunk):
    t = np.arange(chunk)[:, None]
    s = np.arange(chunk)[None, :]
    masks = []
    for lh in range(int(math.log2(chunk))):
        same = (t >> (lh + 1)) == (s >> (lh + 1))
        masks.append(same & (((t >> lh) & 1) == 1) & (((s >> lh) & 1) == 0))
    return np.stack(masks).astype(np.float32)


def _midpoint_rows(cum, h, row):
    n = cum.shape[0]
    if h == 1:
        return jnp.where((row & 1) == 1, pltpu.roll(cum, 1, 0), cum)
    if h == 2:
        pos = row & 3
        nxt = pltpu.roll(cum, n - 1, 0)
        prv = pltpu.roll(cum, 1, 0)
        prv2 = pltpu.roll(cum, 2, 0)
        return jnp.where(pos == 0, nxt, jnp.where(pos == 1, cum, jnp.where(pos == 2, prv, prv2)))
    blocks = cum.reshape(n // (2 * h), 2 * h, cum.shape[1])
    mid = jnp.broadcast_to(blocks[:, h - 1:h, :], blocks.shape)
    return mid.reshape(cum.shape)


def _hgrn_head(q, fl, v, gate, lb, gn, state_t, masks_ref, row):
    n = q.shape[0]
    f = lb + (1.0 - lb) * jax.nn.sigmoid(fl)
    logf = jnp.log(f)
    k = 1.0 - f

    cum = logf
    shift = 1
    while shift < n:
        cum = cum + jnp.where(row >= shift, pltpu.roll(cum, shift, 0), 0.0)
        shift *= 2

    scores = jnp.zeros((n, n), F32)
    for lh in range(int(math.log2(n))):
        h = 1 << lh
        decay = jnp.exp(-jnp.abs(cum - _midpoint_rows(cum, h, row)))
        part = lax.dot_general((q * decay).astype(BF16), (k * decay).astype(BF16),
                               (((1,), (1,)), ((), ())), preferred_element_type=F32)
        scores = scores + part * masks_ref[lh]

    v16 = v.astype(BF16)
    o = jnp.dot(scores.astype(BF16), v16, preferred_element_type=F32)
    o = o + jnp.sum(q * k, axis=-1, keepdims=True) * v
    o = o + lax.dot_general((q * jnp.exp(cum)).astype(BF16), state_t.astype(BF16),
                            (((1,), (1,)), ((), ())), preferred_element_type=F32)

    last = cum[n - 1:n, :]
    new_state_t = state_t * jnp.exp(last) + lax.dot_general(
        v16, (k * jnp.exp(last - cum)).astype(BF16),
        (((0,), (0,)), ((), ())), preferred_element_type=F32)

    y = o * _rms_scale(o) * gn * (gate * jax.nn.sigmoid(gate))
    return y, new_state_t


def _hgrn_body(q_ref, f_ref, i_ref, g_ref, lbl_ref, gn_ref, masks_ref, y_ref, state_ref):
    @pl.when(pl.program_id(1) == 0)
    def _():
        state_ref[...] = jnp.zeros_like(state_ref)

    logits = lbl_ref[...]
    e = jnp.exp(logits - jnp.max(logits, axis=0, keepdims=True))
    lb_all = e[0:1, :] / jnp.sum(e, axis=0, keepdims=True)
    gn = gn_ref[...]
    n = q_ref.shape[0]
    row = lax.broadcasted_iota(jnp.int32, (n, HEAD_DIM), 0)

    for hd in range(q_ref.shape[1] // HEAD_DIM):
        sl = slice(hd * HEAD_DIM, (hd + 1) * HEAD_DIM)
        y, new_state = _hgrn_head(q_ref[:, sl], f_ref[:, sl], i_ref[:, sl], g_ref[:, sl],
                                  lb_all[:, sl], gn, state_ref[hd], masks_ref, row)
        state_ref[hd] = new_state
        y_ref[:, sl] = y.astype(y_ref.dtype)


def _hgrn(proj, lb_logits, g_norm, d_hgrn):
    b, t, _ = proj.shape
    heads = d_hgrn // HEAD_DIM
    masks = jnp.asarray(_level_masks(HGRN_CHUNK))
    col = lambda kk: pl.BlockSpec((None, HGRN_CHUNK, d_hgrn), lambda bi, ci, kk=kk: (bi, ci, kk))
    return pl.pallas_call(
        _hgrn_body,
        grid=(b, t // HGRN_CHUNK),
        in_specs=[
            col(0), col(1), col(2), col(3),
            pl.BlockSpec(lb_logits.shape, lambda bi, ci: (0, 0)),
            pl.BlockSpec((1, HEAD_DIM), lambda bi, ci: (0, 0)),
            pl.BlockSpec(masks.shape, lambda bi, ci: (0, 0, 0)),
        ],
        out_specs=pl.BlockSpec((None, HGRN_CHUNK, d_hgrn), lambda bi, ci: (bi, ci, 0)),
        out_shape=jax.ShapeDtypeStruct((b, t, d_hgrn), BF16),
        scratch_shapes=[pltpu.VMEM((heads, HEAD_DIM, HEAD_DIM), F32)],
        compiler_params=_params(("arbitrary", "arbitrary")),
        name="hgrn",
    )(proj, proj, proj, proj, lb_logits, g_norm, masks)


def _mixout_body(yrec_ref, u_ref, halo_ref, wpool_ref, pscale_ref, wout_ref, post_g_ref, x_ref, o_ref):
    ti = pl.program_id(1)
    d_rec = yrec_ref.shape[1]
    group = u_ref.shape[1] // len(POOL_WINDOWS)

    for r in range(u_ref.shape[0] // MIXOUT_ROW_BLOCK):
        start = r * MIXOUT_ROW_BLOCK
        rows = pl.ds(start, MIXOUT_ROW_BLOCK)
        if r == 0:
            halo = jnp.where(ti > 0, halo_ref[...], 0.0)
        else:
            halo = u_ref[pl.ds(start - POOL_HALO, POOL_HALO), :]
        ext = jnp.concatenate([halo, u_ref[rows, :]], axis=0)
        pos = ti * u_ref.shape[0] + start + lax.broadcasted_iota(jnp.int32, (MIXOUT_ROW_BLOCK, 1), 0)

        mixed = []
        for gi, w in enumerate(POOL_WINDOWS):
            seg = ext[:, gi * group:(gi + 1) * group]
            acc = seg
            span = 1
            while span < w:
                acc = acc + pltpu.roll(acc, span, 0)
                span *= 2
            count = jnp.minimum(pos + 1, w).astype(F32)
            pooled = acc[POOL_HALO:, :] / count - seg[POOL_HALO:, :]
            mixed.append(jnp.dot(pooled.astype(BF16), wpool_ref[gi], preferred_element_type=F32))
        y_pool = jnp.concatenate(mixed, axis=-1) * pscale_ref[...]

        y = jnp.dot(yrec_ref[rows, :], wout_ref[0:d_rec, :], preferred_element_type=F32)
        y = y + jnp.dot(y_pool.astype(BF16), wout_ref[d_rec:, :], preferred_element_type=F32)
        o_ref[rows, :] = x_ref[rows, :] + y * _rms_scale(y) * post_g_ref[...]


def _mixout(yrec, proj, w_pool, pool_scale, w_out, post_g, x, tokens_per_seq):
    n, d = x.shape
    d_rec = yrec.shape[1]
    d_pool = pool_scale.shape[1]
    u_col = (proj.shape[1] - d_pool) // d_pool
    tiles = tokens_per_seq // MIXOUT_ROWS
    halo_per_tile = MIXOUT_ROWS // POOL_HALO
    return pl.pallas_call(
        _mixout_body,
        grid=(n // tokens_per_seq, tiles),
        in_specs=[
            pl.BlockSpec((MIXOUT_ROWS, d_rec), lambda bi, ti: (bi * tiles + ti, 0)),
            pl.BlockSpec((MIXOUT_ROWS, d_pool), lambda bi, ti: (bi * tiles + ti, u_col)),
            pl.BlockSpec((POOL_HALO, d_pool),
                         lambda bi, ti: (jnp.maximum((bi * tiles + ti) * halo_per_tile - 1, 0), u_col)),
            _resident(w_pool.shape),
            _resident((1, d_pool)),
            _resident(w_out.shape),
            _resident((1, d)),
            pl.BlockSpec((MIXOUT_ROWS, d), lambda bi, ti: (bi * tiles + ti, 0)),
        ],
        out_specs=pl.BlockSpec((MIXOUT_ROWS, d), lambda bi, ti: (bi * tiles + ti, 0)),
        out_shape=jax.ShapeDtypeStruct((n, d), F32),
        compiler_params=_params(("parallel", "arbitrary")),
        name="mixout",
    )(yrec, proj, proj, w_pool, pool_scale, w_out, post_g, x)


def kernel(x, ffn1_pre_g, ffn1_w_gate, ffn1_w_up, ffn1_w_down, ffn1_post_g, mix_pre_g, w_in, hgrn_lb_logits, hgrn_g_norm, w_pool, pool_scale, w_out, mix_post_g, ffn2_pre_g, ffn2_w_gate, ffn2_w_up, ffn2_w_down, ffn2_post_g):
    b, t, d = x.shape
    n = b * t
    d_hgrn = hgrn_lb_logits.shape[1]
    depth = w_in.shape[0]
    assert depth == 1 and hgrn_lb_logits.shape[0] == depth + 1
    assert w_in.shape[2] == 4 * d_hgrn + pool_scale.shape[1]
    assert t % HGRN_CHUNK == 0 and t % MIXOUT_ROWS == 0 and n % FFN_ROWS == 0

    h = x.reshape(n, d)
    h = _ffn(h, ffn1_pre_g, ffn1_w_gate[0].astype(BF16), ffn1_w_up[0].astype(BF16),
             ffn1_w_down[0].astype(BF16), ffn1_post_g)
    proj = _inproj(h, mix_pre_g, w_in[0].astype(BF16))
    y_rec = _hgrn(proj.reshape(b, t, -1), hgrn_lb_logits, hgrn_g_norm, d_hgrn)
    h = _mixout(y_rec.reshape(n, d_hgrn), proj, w_pool[0].astype(BF16), pool_scale,
                w_out[0].astype(BF16), mix_post_g, h, t)
    h = _ffn(h, ffn2_pre_g, ffn2_w_gate[0].astype(BF16), ffn2_w_up[0].astype(BF16),
             ffn2_w_down[0].astype(BF16), ffn2_post_g)
    return h.reshape(b, t, d)
```

```python
import functools
import math

import numpy as np
import jax
import jax.numpy as jnp
from jax import lax
from jax.experimental import pallas as pl
from jax.experimental.pallas import tpu as pltpu

RMS_EPS = 1e-6
HEAD_DIM = 128
SUBLANES = 8
POOL_WINDOWS = (2, 4, 8, 16)
POOL_HALO = 16
V7X_VMEM_LIMIT_BYTES = 56 * 1024 * 1024
LOG2E = 1.4426950408889634

FFN_ROWS = 1024
FFN_COLS = 512
FFN_ROW_BLOCK = 512
INPROJ_ROWS = 512
INPROJ_ROW_BLOCK = 256
HGRN_CHUNK = 128
HGRN_ROWS = 512
HGRN_HEAD_GROUP = 4
MIXOUT_ROWS = 512
MIXOUT_ROW_BLOCK = 256

BF16 = jnp.bfloat16
F32 = jnp.float32


def _rms_scale(v):
    return lax.rsqrt(jnp.mean(v * v, axis=-1, keepdims=True) + RMS_EPS)


def _params(semantics):
    return pltpu.CompilerParams(dimension_semantics=semantics,
                                vmem_limit_bytes=V7X_VMEM_LIMIT_BYTES)


def _resident(shape):
    return pl.BlockSpec(shape, lambda *_: (0,) * len(shape), pipeline_mode=pl.Buffered(1))


def _nt_dot(a, b):
    return lax.dot_general(a, b, (((1,), (1,)), ((), ())), preferred_element_type=F32)


def _ffn_step(x_ref, pre_g_ref, wg_ref, wu_ref, wd_ref, post_g_ref, o_ref, hn_ref, *, first, last):
    for r in range(x_ref.shape[0] // FFN_ROW_BLOCK):
        rows = pl.ds(r * FFN_ROW_BLOCK, FFN_ROW_BLOCK)
        if first:
            x = x_ref[rows, :]
            hn = (x * _rms_scale(x) * pre_g_ref[...]).astype(BF16)
            hn_ref[rows, :] = hn
        else:
            hn = hn_ref[rows, :]
        g = jnp.dot(hn, wg_ref[...], preferred_element_type=F32)
        u = jnp.dot(hn, wu_ref[...], preferred_element_type=F32)
        a = (g * jax.nn.sigmoid(g) * u).astype(BF16)
        acc = jnp.dot(a, wd_ref[...], preferred_element_type=F32)
        if not first:
            acc = o_ref[rows, :] + acc
        if last:
            acc = x_ref[rows, :] + 0.5 * (acc * _rms_scale(acc) * post_g_ref[...])
        o_ref[rows, :] = acc


def _ffn_body(*refs):
    j = pl.program_id(1)
    last_j = pl.num_programs(1) - 1
    pl.when(j == 0)(functools.partial(_ffn_step, *refs, first=True, last=False))
    pl.when((j > 0) & (j < last_j))(functools.partial(_ffn_step, *refs, first=False, last=False))
    pl.when(j == last_j)(functools.partial(_ffn_step, *refs, first=False, last=True))


def _ffn(x, pre_g, wg, wu, wd, post_g):
    n, d = x.shape
    dff = wg.shape[1]
    grid = (n // FFN_ROWS, dff // FFN_COLS)
    return pl.pallas_call(
        _ffn_body,
        grid=grid,
        in_specs=[
            pl.BlockSpec((FFN_ROWS, d), lambda i, j: (i, 0)),
            pl.BlockSpec((1, d), lambda i, j: (0, 0)),
            pl.BlockSpec((d, FFN_COLS), lambda i, j: (0, j)),
            pl.BlockSpec((d, FFN_COLS), lambda i, j: (0, j)),
            pl.BlockSpec((FFN_COLS, d), lambda i, j: (j, 0)),
            pl.BlockSpec((1, d), lambda i, j: (0, 0)),
        ],
        out_specs=pl.BlockSpec((FFN_ROWS, d), lambda i, j: (i, 0)),
        out_shape=jax.ShapeDtypeStruct((n, d), F32),
        scratch_shapes=[pltpu.VMEM((FFN_ROWS, d), BF16)],
        compiler_params=_params(("parallel", "arbitrary")),
        name="ffn",
    )(x, pre_g, wg, wu, wd, post_g)


def _inproj_body(x_ref, g_ref, w_ref, o_ref):
    for r in range(x_ref.shape[0] // INPROJ_ROW_BLOCK):
        rows = pl.ds(r * INPROJ_ROW_BLOCK, INPROJ_ROW_BLOCK)
        x = x_ref[rows, :]
        hn = (x * _rms_scale(x) * g_ref[...]).astype(BF16)
        o_ref[rows, :] = jnp.dot(hn, w_ref[...], preferred_element_type=F32)


def _inproj(x, gain, w):
    n, d = x.shape
    dout = w.shape[1]
    return pl.pallas_call(
        _inproj_body,
        grid=(n // INPROJ_ROWS,),
        in_specs=[
            pl.BlockSpec((INPROJ_ROWS, d), lambda i: (i, 0)),
            _resident((1, d)),
            _resident(w.shape),
        ],
        out_specs=pl.BlockSpec((INPROJ_ROWS, dout), lambda i: (i, 0)),
        out_shape=jax.ShapeDtypeStruct((n, dout), F32),
        compiler_params=_params(("parallel",)),
        name="inproj",
    )(x, gain, w)


def _hgrn_constants(chunk):
    t = np.arange(chunk)[:, None]
    s = np.arange(chunk)[None, :]
    tri = (s <= t).astype(np.float32)
    small = []
    for lh in range(int(math.log2(SUBLANES))):
        same = (t >> (lh + 1)) == (s >> (lh + 1))
        small.append(same & (((t >> lh) & 1) == 1) & (((s >> lh) & 1) == 0))
    return tri, np.stack(small).astype(np.float32)


def _assemble_scores(near, far, lane):
    n = near.shape[0]
    score_rows = []
    for t0 in range(0, n, SUBLANES):
        acc = None
        h = n // 2
        while h >= SUBLANES:
            if t0 & h:
                bs = t0 & ~(2 * h - 1)
                at = (bs // (2 * h)) * h + (t0 - bs - h)
                piece = far[h][at:at + SUBLANES, :]
                acc = piece if acc is None else jnp.where(lane >= bs, piece, acc)
            h //= 2
        own = near[t0:t0 + SUBLANES, :]
        score_rows.append(own if acc is None else jnp.where(lane >= t0, own, acc))
    return jnp.concatenate(score_rows, axis=0)


def _hgrn_body(q_ref, f_ref, i_ref, g_ref, lbl_ref, gn_ref, tri_ref, small_ref, y_ref, state_ref):
    @pl.when(pl.program_id(1) == 0)
    def _():
        state_ref[...] = jnp.zeros_like(state_ref)

    logits = lbl_ref[...]
    e = jnp.exp(logits - jnp.max(logits, axis=0, keepdims=True))
    lb = e[0:1, :] / jnp.sum(e, axis=0, keepdims=True)
    tri = tri_ref[...]
    gn = gn_ref[...]

    width = HGRN_HEAD_GROUP * HEAD_DIM
    for start in range(0, q_ref.shape[0], HGRN_CHUNK):
        rows = pl.ds(start, HGRN_CHUNK)
        for first_head in range(0, q_ref.shape[1] // HEAD_DIM, HGRN_HEAD_GROUP):
            span = slice(first_head * HEAD_DIM, first_head * HEAD_DIM + width)
            y_ref[rows, span] = _hgrn_heads(
                q_ref[rows, span], f_ref[rows, span], i_ref[rows, span], g_ref[rows, span],
                lb[:, span], gn, tri, small_ref, state_ref, first_head).astype(y_ref.dtype)


def _hgrn_heads(q, f_logit, v, gate, lb, gn, tri, small_ref, state_ref, first_head):
    n, width = q.shape
    groups = n // SUBLANES
    head_cols = [slice(c, c + HEAD_DIM) for c in range(0, width, HEAD_DIM)]

    f = lb + (1.0 - lb) * jax.nn.sigmoid(f_logit)
    k = 1.0 - f
    lg = jnp.log(f) * LOG2E

    g1 = lg.astype(BF16)
    r1 = lg - g1.astype(F32)
    g2 = r1.astype(BF16)
    g3 = (r1 - g2.astype(F32)).astype(BF16)
    cum = (jnp.dot(tri, g1, preferred_element_type=F32) + jnp.dot(tri, g2, preferred_element_type=F32)
           + jnp.dot(tri, g3, preferred_element_type=F32))

    sub = lax.broadcasted_iota(jnp.int32, (groups, SUBLANES, width), 1)
    c3 = cum.reshape(groups, SUBLANES, width)
    q3 = q.reshape(groups, SUBLANES, width)
    k3 = k.reshape(groups, SUBLANES, width)

    def row_of(i):
        return jnp.broadcast_to(c3[:, i:i + 1, :], c3.shape)

    mids = (jnp.where((sub & 1) == 1, pltpu.roll(c3, 1, 1), c3),
            jnp.where(sub < 4, row_of(1), row_of(5)),
            row_of(3))
    near_rows = []
    for lh, mid in enumerate(mids):
        scaled = jnp.where((sub & (1 << lh)) != 0, q3, k3) * jnp.exp2(-jnp.abs(c3 - mid))
        near_rows.append(scaled.reshape(n, width).astype(BF16))

    far_q, far_k = {}, {}
    h = SUBLANES
    while h < n:
        starts = range(0, n, 2 * h)
        args, rows = [], []
        for bs in starts:
            mid = cum[bs + h - 1:bs + h, :]
            args += [mid - cum[bs:bs + h, :], cum[bs + h:bs + 2 * h, :] - mid]
            rows += [k[bs:bs + h, :], q[bs + h:bs + 2 * h, :]]
        scaled = jnp.concatenate(rows, axis=0) * jnp.exp2(jnp.concatenate(args, axis=0))
        zeros = jnp.zeros((h, width), F32)
        far_q[h] = jnp.concatenate([scaled[bs + h:bs + 2 * h, :] for bs in starts], axis=0).astype(BF16)
        far_k[h] = jnp.concatenate([blk for bs in starts for blk in (scaled[bs:bs + h, :], zeros)],
                                   axis=0).astype(BF16)
        h *= 2

    last = cum[n - 1:n, :]
    q_in = (q * jnp.exp2(cum)).astype(BF16)
    k_out = (k * jnp.exp2(last - cum)).astype(BF16)
    carry = jnp.exp2(last)
    diag = q * k
    v16 = v.astype(BF16)

    lane = lax.broadcasted_iota(jnp.int32, (SUBLANES, n), 1)
    outs = []
    for hd, cols in enumerate(head_cols):
        near = None
        for lh, rows16 in enumerate(near_rows):
            part = _nt_dot(rows16[:, cols], rows16[:, cols]) * small_ref[lh]
            near = part if near is None else near + part
        far = {h: _nt_dot(far_q[h][:, cols], far_k[h][:, cols]) for h in far_q}
        scores = _assemble_scores(near, far, lane)

        state_t = state_ref[first_head + hd]
        o = jnp.dot(scores.astype(BF16), v16[:, cols], preferred_element_type=F32)
        o = o + jnp.sum(diag[:, cols], axis=-1, keepdims=True) * v[:, cols]
        o = o + _nt_dot(q_in[:, cols], state_t.astype(BF16))
        state_ref[first_head + hd] = state_t * carry[:, cols] + lax.dot_general(
            v16[:, cols], k_out[:, cols], (((0,), (0,)), ((), ())), preferred_element_type=F32)
        outs.append(o * _rms_scale(o) * gn)

    return jnp.concatenate(outs, axis=-1) * (gate * jax.nn.sigmoid(gate))


def _hgrn(proj, lb_logits, g_norm, d_hgrn):
    b, t, _ = proj.shape
    heads = d_hgrn // HEAD_DIM
    tri, small = _hgrn_constants(HGRN_CHUNK)
    tri = jnp.asarray(tri, dtype=BF16)
    small = jnp.asarray(small)
    col = lambda kk: pl.BlockSpec((None, HGRN_ROWS, d_hgrn), lambda bi, ci, kk=kk: (bi, ci, kk))
    return pl.pallas_call(
        _hgrn_body,
        grid=(b, t // HGRN_ROWS),
        in_specs=[
            col(0), col(1), col(2), col(3),
            _resident(lb_logits.shape),
            _resident((1, HEAD_DIM)),
            _resident(tri.shape),
            _resident(small.shape),
        ],
        out_specs=pl.BlockSpec((None, HGRN_ROWS, d_hgrn), lambda bi, ci: (bi, ci, 0)),
        out_shape=jax.ShapeDtypeStruct((b, t, d_hgrn), BF16),
        scratch_shapes=[pltpu.VMEM((heads, HEAD_DIM, HEAD_DIM), F32)],
        compiler_params=_params(("arbitrary", "arbitrary")),
        name="hgrn",
    )(proj, proj, proj, proj, lb_logits, g_norm, tri, small)


def _mixout_body(yrec_ref, u_ref, halo_ref, wpool_ref, pscale_ref, wout_ref, post_g_ref, x_ref, o_ref):
    ti = pl.program_id(1)
    d_rec = yrec_ref.shape[1]
    group = u_ref.shape[1] // len(POOL_WINDOWS)

    for r in range(u_ref.shape[0] // MIXOUT_ROW_BLOCK):
        start = r * MIXOUT_ROW_BLOCK
        rows = pl.ds(start, MIXOUT_ROW_BLOCK)
        if r == 0:
            halo = jnp.where(ti > 0, halo_ref[...], 0.0)
        else:
            halo = u_ref[pl.ds(start - POOL_HALO, POOL_HALO), :]
        ext = jnp.concatenate([halo, u_ref[rows, :]], axis=0)
        pos = ti * u_ref.shape[0] + start + lax.broadcasted_iota(jnp.int32, (MIXOUT_ROW_BLOCK, 1), 0)

        mixed = []
        for gi, w in enumerate(POOL_WINDOWS):
            seg = ext[:, gi * group:(gi + 1) * group]
            acc = seg
            span = 1
            while span < w:
                acc = acc + pltpu.roll(acc, span, 0)
                span *= 2
            count = jnp.minimum(pos + 1, w).astype(F32)
            pooled = acc[POOL_HALO:, :] / count - seg[POOL_HALO:, :]
            mixed.append(jnp.dot(pooled.astype(BF16), wpool_ref[gi], preferred_element_type=F32))
        y_pool = jnp.concatenate(mixed, axis=-1) * pscale_ref[...]

        y = jnp.dot(yrec_ref[rows, :], wout_ref[0:d_rec, :], preferred_element_type=F32)
        y = y + jnp.dot(y_pool.astype(BF16), wout_ref[d_rec:, :], preferred_element_type=F32)
        o_ref[rows, :] = x_ref[rows, :] + y * _rms_scale(y) * post_g_ref[...]


def _mixout(yrec, proj, w_pool, pool_scale, w_out, post_g, x, tokens_per_seq):
    n, d = x.shape
    d_rec = yrec.shape[1]
    d_pool = pool_scale.shape[1]
    u_col = (proj.shape[1] - d_pool) // d_pool
    tiles = tokens_per_seq // MIXOUT_ROWS
    halo_per_tile = MIXOUT_ROWS // POOL_HALO
    return pl.pallas_call(
        _mixout_body,
        grid=(n // tokens_per_seq, tiles),
        in_specs=[
            pl.BlockSpec((MIXOUT_ROWS, d_rec), lambda bi, ti: (bi * tiles + ti, 0)),
            pl.BlockSpec((MIXOUT_ROWS, d_pool), lambda bi, ti: (bi * tiles + ti, u_col)),
            pl.BlockSpec((POOL_HALO, d_pool),
                         lambda bi, ti: (jnp.maximum((bi * tiles + ti) * halo_per_tile - 1, 0), u_col)),
            _resident(w_pool.shape),
            _resident((1, d_pool)),
            _resident(w_out.shape),
            _resident((1, d)),
            pl.BlockSpec((MIXOUT_ROWS, d), lambda bi, ti: (bi * tiles + ti, 0)),
        ],
        out_specs=pl.BlockSpec((MIXOUT_ROWS, d), lambda bi, ti: (bi * tiles + ti, 0)),
        out_shape=jax.ShapeDtypeStruct((n, d), F32),
        compiler_params=_params(("parallel", "arbitrary")),
        name="mixout",
    )(yrec, proj, proj, w_pool, pool_scale, w_out, post_g, x)


def kernel(x, ffn1_pre_g, ffn1_w_gate, ffn1_w_up, ffn1_w_down, ffn1_post_g, mix_pre_g, w_in, hgrn_lb_logits, hgrn_g_norm, w_pool, pool_scale, w_out, mix_post_g, ffn2_pre_g, ffn2_w_gate, ffn2_w_up, ffn2_w_down, ffn2_post_g):
    b, t, d = x.shape
    n = b * t
    d_hgrn = hgrn_lb_logits.shape[1]
    depth = w_in.shape[0]
    assert depth == 1 and hgrn_lb_logits.shape[0] == depth + 1
    assert w_in.shape[2] == 4 * d_hgrn + pool_scale.shape[1]
    assert t % HGRN_ROWS == 0 and t % MIXOUT_ROWS == 0 and n % FFN_ROWS == 0

    h = x.reshape(n, d)
    h = _ffn(h, ffn1_pre_g, ffn1_w_gate[0].astype(BF16), ffn1_w_up[0].astype(BF16),
             ffn1_w_down[0].astype(BF16), ffn1_post_g)
    proj = _inproj(h, mix_pre_g, w_in[0].astype(BF16))
    y_rec = _hgrn(proj.reshape(b, t, -1), hgrn_lb_logits, hgrn_g_norm, d_hgrn)
    h = _mixout(y_rec.reshape(n, d_hgrn), proj, w_pool[0].astype(BF16), pool_scale,
                w_out[0].astype(BF16), mix_post_g, h, t)
    h = _ffn(h, ffn2_pre_g, ffn2_w_gate[0].astype(BF16), ffn2_w_up[0].astype(BF16),
             ffn2_w_down[0].astype(BF16), ffn2_post_g)
    return h.reshape(b, t, d)
```

```python
import functools
import math

import numpy as np
import jax
import jax.numpy as jnp
from jax import lax
from jax.experimental import pallas as pl
from jax.experimental.pallas import tpu as pltpu

RMS_EPS = 1e-6
HEAD_DIM = 128
SUBLANES = 8
BF16_SUBLANES = 16
POOL_WINDOWS = (2, 4, 8, 16)
POOL_HALO = 16
V7X_VMEM_LIMIT_BYTES = 56 * 1024 * 1024
LOG2E = 1.4426950408889634

FFN_ROWS = 1024
FFN_COLS = 512
FFN_ROW_BLOCK = 512
INPROJ_ROWS = 512
INPROJ_ROW_BLOCK = 256
HGRN_CHUNK = 128
HGRN_ROWS = 512
HGRN_HEAD_GROUP = 4
MIXOUT_ROWS = 1024
MIXOUT_ROW_BLOCK = 256

BF16 = jnp.bfloat16
F32 = jnp.float32


def _rms_scale(v):
    return lax.rsqrt(jnp.mean(v * v, axis=-1, keepdims=True) + RMS_EPS)


def _params(semantics):
    return pltpu.CompilerParams(dimension_semantics=semantics,
                                vmem_limit_bytes=V7X_VMEM_LIMIT_BYTES)


def _resident(shape):
    return pl.BlockSpec(shape, lambda *_: (0,) * len(shape), pipeline_mode=pl.Buffered(1))


def _cast_blocks(w, steps):
    repeat = 1
    while (w.shape[0] * repeat) % steps or (w.shape[0] * repeat // steps) % BF16_SUBLANES:
        repeat *= 2
        assert repeat <= steps, (w.shape, steps)
    return w.shape[0] * repeat // steps, repeat


def _cast_side_inputs(src_refs, dst_refs, fan_in):
    src_refs = list(src_refs)
    for dst, count in zip(dst_refs, fan_in):
        srcs = [src_refs.pop(0) for _ in range(count)]
        if count == 1:
            dst[...] = srcs[0][...].astype(dst.dtype)
            continue
        for c in range(srcs[0].shape[1] // FFN_COLS):
            for k, src in enumerate(srcs):
                dst[:, pl.ds((c * count + k) * FFN_COLS, FFN_COLS)] = (
                    src[:, pl.ds(c * FFN_COLS, FFN_COLS)].astype(dst.dtype))


def _sigmoid(v):
    return 1.0 / (1.0 + jnp.exp2(v * (-LOG2E)))


def _nt_dot(a, b):
    return lax.dot_general(a, b, (((1,), (1,)), ((), ())), preferred_element_type=F32)


def _ffn_step(x_ref, pre_g_ref, wgu_ref, wd_ref, post_g_ref, o_ref, hn_ref, *, first, last):
    for r in range(x_ref.shape[0] // FFN_ROW_BLOCK):
        rows = pl.ds(r * FFN_ROW_BLOCK, FFN_ROW_BLOCK)
        if first:
            x = x_ref[rows, :]
            hn = (x * _rms_scale(x) * pre_g_ref[...]).astype(BF16)
            hn_ref[rows, :] = hn
        else:
            hn = hn_ref[rows, :]
        gu = jnp.dot(hn, wgu_ref[...], preferred_element_type=F32)
        g, u = gu[:, :FFN_COLS], gu[:, FFN_COLS:]
        a = (g * jax.nn.sigmoid(g) * u).astype(BF16)
        acc = jnp.dot(a, wd_ref[...], preferred_element_type=F32)
        if not first:
            acc = o_ref[rows, :] + acc
        if last:
            acc = x_ref[rows, :] + 0.5 * (acc * _rms_scale(acc) * post_g_ref[...])
        o_ref[rows, :] = acc


def _ffn_body(*refs):
    j = pl.program_id(1)
    last_j = pl.num_programs(1) - 1
    pl.when(j == 0)(functools.partial(_ffn_step, *refs, first=True, last=False))
    pl.when((j > 0) & (j < last_j))(functools.partial(_ffn_step, *refs, first=False, last=False))
    pl.when(j == last_j)(functools.partial(_ffn_step, *refs, first=False, last=True))


def _interleave_gate_up(wg, wu):
    d, dff = wg.shape
    chunks = dff // FFN_COLS
    return jnp.concatenate([wg.reshape(d, chunks, FFN_COLS), wu.reshape(d, chunks, FFN_COLS)],
                           axis=2).reshape(d, 2 * dff)


def _ffn(x, pre_g, w_gu, wd, post_g):
    n, d = x.shape
    dff = wd.shape[0]
    grid = (n // FFN_ROWS, dff // FFN_COLS)
    return pl.pallas_call(
        _ffn_body,
        grid=grid,
        in_specs=[
            pl.BlockSpec((FFN_ROWS, d), lambda i, j: (i, 0)),
            pl.BlockSpec((1, d), lambda i, j: (0, 0)),
            pl.BlockSpec((d, 2 * FFN_COLS), lambda i, j: (0, j)),
            pl.BlockSpec((FFN_COLS, d), lambda i, j: (j, 0)),
            pl.BlockSpec((1, d), lambda i, j: (0, 0)),
        ],
        out_specs=pl.BlockSpec((FFN_ROWS, d), lambda i, j: (i, 0)),
        out_shape=jax.ShapeDtypeStruct((n, d), F32),
        scratch_shapes=[pltpu.VMEM((FFN_ROWS, d), BF16)],
        compiler_params=_params(("parallel", "arbitrary")),
        name="ffn",
    )(x, pre_g, w_gu, wd, post_g)


def _pool(u, halo, first_pos):
    rows = u.shape[0]
    group = u.shape[1] // len(POOL_WINDOWS)
    ext = jnp.concatenate([halo, u], axis=0)
    pos = first_pos + lax.broadcasted_iota(jnp.int32, (rows, 1), 0)
    pooled = []
    for gi, w in enumerate(POOL_WINDOWS):
        seg = ext[:, gi * group:(gi + 1) * group]
        acc = seg
        span = 1
        while span < w:
            acc = acc + pltpu.roll(acc, span, 0)
            span *= 2
        count = jnp.minimum(pos + 1, w).astype(F32)
        pooled.append(acc[POOL_HALO:, :] / count - seg[POOL_HALO:, :])
    return jnp.concatenate(pooled, axis=-1)


def _inproj_body(x_ref, g_ref, w_ref, o_ref, pooled_ref, halo_ref, *, tiles_per_seq):
    tile = pl.program_id(0) % tiles_per_seq
    d_rec = o_ref.shape[1]

    @pl.when(tile == 0)
    def _():
        halo_ref[...] = jnp.zeros_like(halo_ref)

    halo = halo_ref[...]
    for r in range(x_ref.shape[0] // INPROJ_ROW_BLOCK):
        start = r * INPROJ_ROW_BLOCK
        rows = pl.ds(start, INPROJ_ROW_BLOCK)
        x = x_ref[rows, :]
        hn = (x * _rms_scale(x) * g_ref[...]).astype(BF16)
        o_ref[rows, :] = jnp.dot(hn, w_ref[:, :d_rec], preferred_element_type=F32)
        u = jnp.dot(hn, w_ref[:, d_rec:], preferred_element_type=F32)
        pooled_ref[rows, :] = _pool(u, halo, tile * x_ref.shape[0] + start).astype(pooled_ref.dtype)
        halo = u[INPROJ_ROW_BLOCK - POOL_HALO:, :]
    halo_ref[...] = halo


def _inproj(x, gain, w, d_pool, tokens_per_seq):
    n, d = x.shape
    d_rec = w.shape[1] - d_pool
    return pl.pallas_call(
        functools.partial(_inproj_body, tiles_per_seq=tokens_per_seq // INPROJ_ROWS),
        grid=(n // INPROJ_ROWS,),
        in_specs=[
            pl.BlockSpec((INPROJ_ROWS, d), lambda i: (i, 0)),
            _resident((1, d)),
            _resident(w.shape),
        ],
        out_specs=[pl.BlockSpec((INPROJ_ROWS, d_rec), lambda i: (i, 0)),
                   pl.BlockSpec((INPROJ_ROWS, d_pool), lambda i: (i, 0))],
        out_shape=[jax.ShapeDtypeStruct((n, d_rec), F32),
                   jax.ShapeDtypeStruct((n, d_pool), BF16)],
        scratch_shapes=[pltpu.VMEM((POOL_HALO, d_pool), F32)],
        compiler_params=_params(("arbitrary",)),
        name="inproj",
    )(x, gain, w)


def _hgrn_constants(chunk):
    t = np.arange(chunk)[:, None]
    s = np.arange(chunk)[None, :]
    tri = (s <= t).astype(np.float32)
    small = []
    for lh in range(int(math.log2(SUBLANES))):
        same = (t >> (lh + 1)) == (s >> (lh + 1))
        small.append(same & (((t >> lh) & 1) == 1) & (((s >> lh) & 1) == 0))
    return tri, np.stack(small).astype(np.float32)


def _assemble_scores(near, far, lane):
    n = near.shape[0]
    score_rows = []
    for t0 in range(0, n, SUBLANES):
        acc = None
        h = n // 2
        while h >= SUBLANES:
            if t0 & h:
                bs = t0 & ~(2 * h - 1)
                at = (bs // (2 * h)) * h + (t0 - bs - h)
                piece = far[h][at:at + SUBLANES, :]
                acc = piece if acc is None else jnp.where(lane >= bs, piece, acc)
            h //= 2
        own = near[t0:t0 + SUBLANES, :]
        score_rows.append(own if acc is None else jnp.where(lane >= t0, own, acc))
    return jnp.concatenate(score_rows, axis=0)


def _hgrn_body(*refs, fan_in):
    q_ref, f_ref, i_ref, g_ref, lbl_ref, gn_ref, tri_ref, small_ref = refs[:8]
    y_ref = refs[8 + sum(fan_in)]
    state_ref = refs[-1]
    _cast_side_inputs(refs[8:8 + sum(fan_in)], refs[9 + sum(fan_in):-1], fan_in)

    @pl.when(pl.program_id(1) == 0)
    def _():
        state_ref[...] = jnp.zeros_like(state_ref)

    logits = lbl_ref[...]
    e = jnp.exp(logits - jnp.max(logits, axis=0, keepdims=True))
    lb = e[0:1, :] / jnp.sum(e, axis=0, keepdims=True)
    tri = tri_ref[...]
    gn = gn_ref[...]

    width = HGRN_HEAD_GROUP * HEAD_DIM
    for start in range(0, q_ref.shape[0], HGRN_CHUNK):
        rows = pl.ds(start, HGRN_CHUNK)
        for first_head in range(0, q_ref.shape[1] // HEAD_DIM, HGRN_HEAD_GROUP):
            span = slice(first_head * HEAD_DIM, first_head * HEAD_DIM + width)
            y_ref[rows, span] = _hgrn_heads(
                q_ref[rows, span], f_ref[rows, span], i_ref[rows, span], g_ref[rows, span],
                lb[:, span], gn, tri, small_ref, state_ref, first_head).astype(y_ref.dtype)


def _hgrn_heads(q, f_logit, v, gate, lb, gn, tri, small_ref, state_ref, first_head):
    n, width = q.shape
    groups = n // SUBLANES
    head_cols = [slice(c, c + HEAD_DIM) for c in range(0, width, HEAD_DIM)]

    f = lb + (1.0 - lb) * _sigmoid(f_logit)
    k = 1.0 - f
    lg = jnp.log(f) * LOG2E

    g1 = lg.astype(BF16)
    r1 = lg - g1.astype(F32)
    g2 = r1.astype(BF16)
    g3 = (r1 - g2.astype(F32)).astype(BF16)
    cum = (jnp.dot(tri, g1, preferred_element_type=F32) + jnp.dot(tri, g2, preferred_element_type=F32)
           + jnp.dot(tri, g3, preferred_element_type=F32))

    sub = lax.broadcasted_iota(jnp.int32, (groups, SUBLANES, width), 1)
    c3 = cum.reshape(groups, SUBLANES, width)
    q3 = q.reshape(groups, SUBLANES, width)
    k3 = k.reshape(groups, SUBLANES, width)

    def row_of(i):
        return jnp.broadcast_to(c3[:, i:i + 1, :], c3.shape)

    mids = (jnp.where((sub & 1) == 1, pltpu.roll(c3, 1, 1), c3),
            jnp.where(sub < 4, row_of(1), row_of(5)),
            row_of(3))
    near_rows = []
    for lh, mid in enumerate(mids):
        upper = (sub & (1 << lh)) != 0
        decay = jnp.exp2((c3 - mid) * jnp.where(upper, 1.0, -1.0))
        near_rows.append((jnp.where(upper, q3, k3) * decay).reshape(n, width).astype(BF16))

    far_q, far_k = {}, {}
    h = SUBLANES
    while h < n:
        starts = range(0, n, 2 * h)
        args, rows = [], []
        for bs in starts:
            mid = cum[bs + h - 1:bs + h, :]
            args += [mid - cum[bs:bs + h, :], cum[bs + h:bs + 2 * h, :] - mid]
            rows += [k[bs:bs + h, :], q[bs + h:bs + 2 * h, :]]
        scaled = jnp.concatenate(rows, axis=0) * jnp.exp2(jnp.concatenate(args, axis=0))
        zeros = jnp.zeros((h, width), F32)
        far_q[h] = jnp.concatenate([scaled[bs + h:bs + 2 * h, :] for bs in starts], axis=0).astype(BF16)
        far_k[h] = jnp.concatenate([blk for bs in starts for blk in (scaled[bs:bs + h, :], zeros)],
                                   axis=0).astype(BF16)
        h *= 2

    last = cum[n - 1:n, :]
    q_in = (q * jnp.exp2(cum)).astype(BF16)
    k_out = (k * jnp.exp2(last - cum)).astype(BF16)
    carry = jnp.exp2(last)
    diag = q * k
    v16 = v.astype(BF16)

    lane = lax.broadcasted_iota(jnp.int32, (SUBLANES, n), 1)
    outs = []
    for hd, cols in enumerate(head_cols):
        near = None
        for lh, rows16 in enumerate(near_rows):
            part = _nt_dot(rows16[:, cols], rows16[:, cols]) * small_ref[lh]
            near = part if near is None else near + part
        far = {h: _nt_dot(far_q[h][:, cols], far_k[h][:, cols]) for h in far_q}
        scores = _assemble_scores(near, far, lane)

        state_t = state_ref[first_head + hd]
        o = jnp.dot(scores.astype(BF16), v16[:, cols], preferred_element_type=F32)
        o = o + jnp.sum(diag[:, cols], axis=-1, keepdims=True) * v[:, cols]
        o = o + _nt_dot(q_in[:, cols], state_t.astype(BF16))
        state_ref[first_head + hd] = state_t * carry[:, cols] + lax.dot_general(
            v16[:, cols], k_out[:, cols], (((0,), (0,)), ((), ())), preferred_element_type=F32)
        outs.append(o * _rms_scale(o) * gn)

    return jnp.concatenate(outs, axis=-1) * (gate * _sigmoid(gate))


def _hgrn(proj, lb_logits, g_norm, d_hgrn, side_weights):
    b, t, _ = proj.shape
    heads = d_hgrn // HEAD_DIM
    chunks = t // HGRN_ROWS
    tri, small = _hgrn_constants(HGRN_CHUNK)
    tri = jnp.asarray(tri, dtype=BF16)
    small = jnp.asarray(small)
    col = lambda kk: pl.BlockSpec((None, HGRN_ROWS, d_hgrn), lambda bi, ci, kk=kk: (bi, ci, kk))
    src_specs, dst_specs, dst_shapes = [], [], []
    for group in side_weights:
        rows, repeat = _cast_blocks(group[0], b * chunks)
        walk = lambda bi, ci, repeat=repeat: ((bi * chunks + ci) // repeat, 0)
        cols = group[0].shape[1]
        src_specs += [pl.BlockSpec((rows, cols), walk)] * len(group)
        dst_specs.append(pl.BlockSpec((rows, cols * len(group)), walk))
        dst_shapes.append(jax.ShapeDtypeStruct((group[0].shape[0], cols * len(group)), BF16))
    return pl.pallas_call(
        functools.partial(_hgrn_body, fan_in=tuple(len(group) for group in side_weights)),
        grid=(b, chunks),
        in_specs=[
            col(0), col(1), col(2), col(3),
            _resident(lb_logits.shape),
            _resident((1, HEAD_DIM)),
            _resident(tri.shape),
            _resident(small.shape),
        ] + src_specs,
        out_specs=[pl.BlockSpec((None, HGRN_ROWS, d_hgrn), lambda bi, ci: (bi, ci, 0))] + dst_specs,
        out_shape=[jax.ShapeDtypeStruct((b, t, d_hgrn), BF16)] + dst_shapes,
        scratch_shapes=[pltpu.VMEM((heads, HEAD_DIM, HEAD_DIM), F32)],
        compiler_params=_params(("arbitrary", "arbitrary")),
        name="hgrn",
    )(proj, proj, proj, proj, lb_logits, g_norm, tri, small, *[w for group in side_weights for w in group])


def _mixout_body(yrec_ref, pooled_ref, wpool_ref, pscale_ref, wout_ref, post_g_ref, x_ref, o_ref):
    d_rec = yrec_ref.shape[1]
    group = pooled_ref.shape[1] // len(POOL_WINDOWS)
    for r in range(x_ref.shape[0] // MIXOUT_ROW_BLOCK):
        rows = pl.ds(r * MIXOUT_ROW_BLOCK, MIXOUT_ROW_BLOCK)
        mixed = [jnp.dot(pooled_ref[rows, gi * group:(gi + 1) * group], wpool_ref[gi],
                         preferred_element_type=F32) for gi in range(len(POOL_WINDOWS))]
        y_pool = jnp.concatenate(mixed, axis=-1) * pscale_ref[...]
        y = jnp.dot(yrec_ref[rows, :], wout_ref[0:d_rec, :], preferred_element_type=F32)
        y = y + jnp.dot(y_pool.astype(BF16), wout_ref[d_rec:, :], preferred_element_type=F32)
        o_ref[rows, :] = x_ref[rows, :] + y * _rms_scale(y) * post_g_ref[...]


def _mixout(yrec, pooled, w_pool, pool_scale, w_out, post_g, x):
    n, d = x.shape
    row_block = lambda width: pl.BlockSpec((MIXOUT_ROWS, width), lambda i: (i, 0))
    return pl.pallas_call(
        _mixout_body,
        grid=(n // MIXOUT_ROWS,),
        in_specs=[
            row_block(yrec.shape[1]),
            row_block(pooled.shape[1]),
            _resident(w_pool.shape),
            _resident(pool_scale.shape),
            _resident(w_out.shape),
            _resident((1, d)),
            row_block(d),
        ],
        out_specs=row_block(d),
        out_shape=jax.ShapeDtypeStruct((n, d), F32),
        compiler_params=_params(("parallel",)),
        name="mixout",
    )(yrec, pooled, w_pool, pool_scale, w_out, post_g, x)


def kernel(x, ffn1_pre_g, ffn1_w_gate, ffn1_w_up, ffn1_w_down, ffn1_post_g, mix_pre_g, w_in, hgrn_lb_logits, hgrn_g_norm, w_pool, pool_scale, w_out, mix_post_g, ffn2_pre_g, ffn2_w_gate, ffn2_w_up, ffn2_w_down, ffn2_post_g):
    b, t, d = x.shape
    n = b * t
    d_hgrn = hgrn_lb_logits.shape[1]
    depth = w_in.shape[0]
    assert depth == 1 and hgrn_lb_logits.shape[0] == depth + 1
    assert w_in.shape[2] == 4 * d_hgrn + pool_scale.shape[1]
    assert t % HGRN_ROWS == 0 and t % INPROJ_ROWS == 0 and n % MIXOUT_ROWS == 0 and n % FFN_ROWS == 0

    h = x.reshape(n, d)
    h = _ffn(h, ffn1_pre_g, _interleave_gate_up(ffn1_w_gate[0], ffn1_w_up[0]).astype(BF16),
             ffn1_w_down[0].astype(BF16), ffn1_post_g)
    proj, pooled = _inproj(h, mix_pre_g, w_in[0].astype(BF16), pool_scale.shape[1], t)
    y_rec, w_out16, w_gu16, wd16 = _hgrn(
        proj.reshape(b, t, -1), hgrn_lb_logits, hgrn_g_norm, d_hgrn,
        [(w_out[0],), (ffn2_w_gate[0], ffn2_w_up[0]), (ffn2_w_down[0],)])
    h = _mixout(y_rec.reshape(n, d_hgrn), pooled, w_pool[0].astype(BF16), pool_scale,
                w_out16, mix_post_g, h)
    h = _ffn(h, ffn2_pre_g, w_gu16, wd16, ffn2_post_g)
    return h.reshape(b, t, d)
```

```python
import functools
import math

import numpy as np
import jax
import jax.numpy as jnp
from jax import lax
from jax.experimental import pallas as pl
from jax.experimental.pallas import tpu as pltpu

RMS_EPS = 1e-6
HEAD_DIM = 128
SUBLANES = 8
BF16_SUBLANES = 16
POOL_WINDOWS = (2, 4, 8, 16)
POOL_HALO = 16
V7X_VMEM_LIMIT_BYTES = 56 * 1024 * 1024
LOG2E = 1.4426950408889634

FFN_ROWS = 1024
FFN_COLS = 512
FFN_ROW_BLOCK = 512
INPROJ_ROWS = 512
INPROJ_ROW_BLOCK = 256
HGRN_CHUNK = 128
HGRN_ROWS = 512
HGRN_HEAD_GROUP = 4
MIXOUT_ROWS = 1024
MIXOUT_ROW_BLOCK = 256

BF16 = jnp.bfloat16
F32 = jnp.float32


def _rms_scale(v):
    return lax.rsqrt(jnp.mean(v * v, axis=-1, keepdims=True) + RMS_EPS)


def _params(semantics):
    return pltpu.CompilerParams(dimension_semantics=semantics,
                                vmem_limit_bytes=V7X_VMEM_LIMIT_BYTES)


def _resident(shape):
    return pl.BlockSpec(shape, lambda *_: (0,) * len(shape), pipeline_mode=pl.Buffered(1))


def _cast_blocks(w, steps):
    repeat = 1
    while (w.shape[0] * repeat) % steps or (w.shape[0] * repeat // steps) % BF16_SUBLANES:
        repeat *= 2
        assert repeat <= steps, (w.shape, steps)
    return w.shape[0] * repeat // steps, repeat


def _cast_side_inputs(src_refs, dst_refs):
    for src, dst in zip(src_refs, dst_refs):
        dst[...] = src[...].astype(dst.dtype)


def _sigmoid(v):
    return 1.0 / (1.0 + jnp.exp2(v * (-LOG2E)))


def _nt_dot(a, b):
    return lax.dot_general(a, b, (((1,), (1,)), ((), ())), preferred_element_type=F32)


def _ffn_step(x_ref, pre_g_ref, wg_ref, wu_ref, wd_ref, post_g_ref, o_ref, hn_ref, *, first, last):
    for r in range(x_ref.shape[0] // FFN_ROW_BLOCK):
        rows = pl.ds(r * FFN_ROW_BLOCK, FFN_ROW_BLOCK)
        if first:
            x = x_ref[rows, :]
            hn = (x * _rms_scale(x) * pre_g_ref[...]).astype(BF16)
            hn_ref[rows, :] = hn
        else:
            hn = hn_ref[rows, :]
        g = jnp.dot(hn, wg_ref[...], preferred_element_type=F32)
        u = jnp.dot(hn, wu_ref[...], preferred_element_type=F32)
        a = (g * jax.nn.sigmoid(g) * u).astype(BF16)
        acc = jnp.dot(a, wd_ref[...], preferred_element_type=F32)
        if not first:
            acc = o_ref[rows, :] + acc
        if last:
            acc = x_ref[rows, :] + acc * _rms_scale(acc) * (0.5 * post_g_ref[...])
        o_ref[rows, :] = acc


def _ffn_body(*refs):
    j = pl.program_id(1)
    last_j = pl.num_programs(1) - 1
    pl.when(j == 0)(functools.partial(_ffn_step, *refs, first=True, last=False))
    pl.when((j > 0) & (j < last_j))(functools.partial(_ffn_step, *refs, first=False, last=False))
    pl.when(j == last_j)(functools.partial(_ffn_step, *refs, first=False, last=True))


def _ffn(x, pre_g, wg, wu, wd, post_g):
    n, d = x.shape
    dff = wd.shape[0]
    grid = (n // FFN_ROWS, dff // FFN_COLS)
    return pl.pallas_call(
        _ffn_body,
        grid=grid,
        in_specs=[
            pl.BlockSpec((FFN_ROWS, d), lambda i, j: (i, 0)),
            pl.BlockSpec((1, d), lambda i, j: (0, 0)),
            pl.BlockSpec((d, FFN_COLS), lambda i, j: (0, j)),
            pl.BlockSpec((d, FFN_COLS), lambda i, j: (0, j)),
            pl.BlockSpec((FFN_COLS, d), lambda i, j: (j, 0)),
            pl.BlockSpec((1, d), lambda i, j: (0, 0)),
        ],
        out_specs=pl.BlockSpec((FFN_ROWS, d), lambda i, j: (i, 0)),
        out_shape=jax.ShapeDtypeStruct((n, d), F32),
        scratch_shapes=[pltpu.VMEM((FFN_ROWS, d), BF16)],
        compiler_params=_params(("parallel", "arbitrary")),
        name="ffn",
    )(x, pre_g, wg, wu, wd, post_g)


def _pool(u, halo, first_pos):
    rows = u.shape[0]
    group = u.shape[1] // len(POOL_WINDOWS)
    ext = jnp.concatenate([halo, u], axis=0)
    pos = first_pos + lax.broadcasted_iota(jnp.int32, (rows, 1), 0)
    pooled = []
    for gi, w in enumerate(POOL_WINDOWS):
        seg = ext[:, gi * group:(gi + 1) * group]
        acc = seg
        span = 1
        while span < w:
            acc = acc + pltpu.roll(acc, span, 0)
            span *= 2
        count = jnp.minimum(pos + 1, w).astype(F32)
        pooled.append(acc[POOL_HALO:, :] / count - seg[POOL_HALO:, :])
    return jnp.concatenate(pooled, axis=-1)


def _inproj_body(x_ref, g_ref, w_ref, o_ref, pooled_ref, halo_ref, *, tiles_per_seq):
    tile = pl.program_id(0) % tiles_per_seq
    d_rec = o_ref.shape[1]

    @pl.when(tile == 0)
    def _():
        halo_ref[...] = jnp.zeros_like(halo_ref)

    halo = halo_ref[...]
    for r in range(x_ref.shape[0] // INPROJ_ROW_BLOCK):
        start = r * INPROJ_ROW_BLOCK
        rows = pl.ds(start, INPROJ_ROW_BLOCK)
        x = x_ref[rows, :]
        hn = (x * _rms_scale(x) * g_ref[...]).astype(BF16)
        o_ref[rows, :] = jnp.dot(hn, w_ref[:, :d_rec], preferred_element_type=F32)
        u = jnp.dot(hn, w_ref[:, d_rec:], preferred_element_type=F32)
        pooled_ref[rows, :] = _pool(u, halo, tile * x_ref.shape[0] + start).astype(pooled_ref.dtype)
        halo = u[INPROJ_ROW_BLOCK - POOL_HALO:, :]
    halo_ref[...] = halo


def _inproj(x, gain, w, d_pool, tokens_per_seq):
    n, d = x.shape
    d_rec = w.shape[1] - d_pool
    return pl.pallas_call(
        functools.partial(_inproj_body, tiles_per_seq=tokens_per_seq // INPROJ_ROWS),
        grid=(n // INPROJ_ROWS,),
        in_specs=[
            pl.BlockSpec((INPROJ_ROWS, d), lambda i: (i, 0)),
            _resident((1, d)),
            _resident(w.shape),
        ],
        out_specs=[pl.BlockSpec((INPROJ_ROWS, d_rec), lambda i: (i, 0)),
                   pl.BlockSpec((INPROJ_ROWS, d_pool), lambda i: (i, 0))],
        out_shape=[jax.ShapeDtypeStruct((n, d_rec), F32),
                   jax.ShapeDtypeStruct((n, d_pool), BF16)],
        scratch_shapes=[pltpu.VMEM((POOL_HALO, d_pool), F32)],
        compiler_params=_params(("arbitrary",)),
        name="inproj",
    )(x, gain, w)


def _hgrn_constants(chunk):
    t = np.arange(chunk)[:, None]
    s = np.arange(chunk)[None, :]
    tri = (s <= t).astype(np.float32)
    small = []
    for lh in range(1, int(math.log2(SUBLANES))):
        same = (t >> (lh + 1)) == (s >> (lh + 1))
        small.append(same & (((t >> lh) & 1) == 1) & (((s >> lh) & 1) == 0))
    return tri, np.stack(small).astype(np.float32)


def _assemble_scores(near, far, lane):
    n = near.shape[0]
    score_rows = []
    for t0 in range(0, n, SUBLANES):
        acc = None
        h = n // 2
        while h >= SUBLANES:
            if t0 & h:
                bs = t0 & ~(2 * h - 1)
                at = (bs // (2 * h)) * h + (t0 - bs - h)
                piece = far[h][at:at + SUBLANES, :]
                acc = piece if acc is None else jnp.where(lane >= bs, piece, acc)
            h //= 2
        own = near[t0:t0 + SUBLANES, :]
        score_rows.append(own if acc is None else jnp.where(lane >= t0, own, acc))
    return jnp.concatenate(score_rows, axis=0)


def _hgrn_body(*refs, side_casts):
    q_ref, f_ref, i_ref, g_ref, lbl_ref, gn_ref, tri_ref, small_ref = refs[:8]
    y_ref = refs[8 + side_casts]
    state_ref = refs[-1]
    _cast_side_inputs(refs[8:8 + side_casts], refs[9 + side_casts:-1])

    @pl.when(pl.program_id(1) == 0)
    def _():
        state_ref[...] = jnp.zeros_like(state_ref)

    logits = lbl_ref[...]
    e = jnp.exp(logits - jnp.max(logits, axis=0, keepdims=True))
    lb = e[0:1, :] / jnp.sum(e, axis=0, keepdims=True)
    tri = tri_ref[...]
    gn = gn_ref[...]

    width = HGRN_HEAD_GROUP * HEAD_DIM
    for start in range(0, q_ref.shape[0], HGRN_CHUNK):
        rows = pl.ds(start, HGRN_CHUNK)
        for first_head in range(0, q_ref.shape[1] // HEAD_DIM, HGRN_HEAD_GROUP):
            span = slice(first_head * HEAD_DIM, first_head * HEAD_DIM + width)
            y_ref[rows, span] = _hgrn_heads(
                q_ref[rows, span], f_ref[rows, span], i_ref[rows, span], g_ref[rows, span],
                lb[:, span], gn, tri, small_ref, state_ref, first_head).astype(y_ref.dtype)


def _hgrn_heads(q, f_logit, v, gate, lb, gn, tri, small_ref, state_ref, first_head):
    n, width = q.shape
    groups = n // SUBLANES
    head_cols = [slice(c, c + HEAD_DIM) for c in range(0, width, HEAD_DIM)]

    f = lb + (1.0 - lb) * _sigmoid(f_logit)
    k = 1.0 - f
    lg = jnp.log(f) * LOG2E

    g1 = lg.astype(BF16)
    r1 = lg - g1.astype(F32)
    g2 = r1.astype(BF16)
    g3 = (r1 - g2.astype(F32)).astype(BF16)
    cum = (jnp.dot(tri, g1, preferred_element_type=F32) + jnp.dot(tri, g2, preferred_element_type=F32)
           + jnp.dot(tri, g3, preferred_element_type=F32))

    sub = lax.broadcasted_iota(jnp.int32, (groups, SUBLANES, width), 1)
    c3 = cum.reshape(groups, SUBLANES, width)
    q3 = q.reshape(groups, SUBLANES, width)
    k3 = k.reshape(groups, SUBLANES, width)

    def row_of(i):
        return jnp.broadcast_to(c3[:, i:i + 1, :], c3.shape)

    near_rows = {}
    for lh, mid in ((1, jnp.where(sub < 4, row_of(1), row_of(5))), (2, row_of(3))):
        upper = (sub & (1 << lh)) != 0
        decay = jnp.exp2((c3 - mid) * jnp.where(upper, 1.0, -1.0))
        near_rows[lh] = (jnp.where(upper, q3, k3) * decay).reshape(n, width).astype(BF16)

    odd = (sub & 1) == 1
    prev_k = pltpu.roll(k3, 1, 1).reshape(n, width)
    prev_v = jnp.where(odd, pltpu.roll(v.reshape(groups, SUBLANES, width), 1, 1), 0.0).reshape(n, width)
    pair = q * f * prev_k
    diag = q * k

    far_q, far_k = {}, {}
    h = SUBLANES
    while h < n:
        starts = range(0, n, 2 * h)
        args, rows = [], []
        for bs in starts:
            mid = cum[bs + h - 1:bs + h, :]
            args += [mid - cum[bs:bs + h, :], cum[bs + h:bs + 2 * h, :] - mid]
            rows += [k[bs:bs + h, :], q[bs + h:bs + 2 * h, :]]
        scaled = jnp.concatenate(rows, axis=0) * jnp.exp2(jnp.concatenate(args, axis=0))
        zeros = jnp.zeros((h, width), F32)
        far_q[h] = jnp.concatenate([scaled[bs + h:bs + 2 * h, :] for bs in starts], axis=0).astype(BF16)
        far_k[h] = jnp.concatenate([blk for bs in starts for blk in (scaled[bs:bs + h, :], zeros)],
                                   axis=0).astype(BF16)
        h *= 2

    last = cum[n - 1:n, :]
    q_in = (q * jnp.exp2(cum)).astype(BF16)
    k_out = (k * jnp.exp2(last - cum)).astype(BF16)
    carry = jnp.exp2(last)
    v16 = v.astype(BF16)

    lane = lax.broadcasted_iota(jnp.int32, (SUBLANES, n), 1)
    outs = []
    for hd, cols in enumerate(head_cols):
        near = None
        for lh, rows16 in near_rows.items():
            part = _nt_dot(rows16[:, cols], rows16[:, cols]) * small_ref[lh - 1]
            near = part if near is None else near + part
        far = {h: _nt_dot(far_q[h][:, cols], far_k[h][:, cols]) for h in far_q}
        scores = _assemble_scores(near, far, lane)

        state_t = state_ref[first_head + hd]
        o = jnp.dot(scores.astype(BF16), v16[:, cols], preferred_element_type=F32)
        o = o + jnp.sum(diag[:, cols], axis=-1, keepdims=True) * v[:, cols]
        o = o + jnp.sum(pair[:, cols], axis=-1, keepdims=True) * prev_v[:, cols]
        o = o + _nt_dot(q_in[:, cols], state_t.astype(BF16))
        state_ref[first_head + hd] = state_t * carry[:, cols] + lax.dot_general(
            v16[:, cols], k_out[:, cols], (((0,), (0,)), ((), ())), preferred_element_type=F32)
        outs.append(o * _rms_scale(o) * gn)

    return jnp.concatenate(outs, axis=-1) * (gate * _sigmoid(gate))


def _hgrn(proj, lb_logits, g_norm, d_hgrn, side_weights):
    b, t, _ = proj.shape
    heads = d_hgrn // HEAD_DIM
    chunks = t // HGRN_ROWS
    tri, small = _hgrn_constants(HGRN_CHUNK)
    tri = jnp.asarray(tri, dtype=BF16)
    small = jnp.asarray(small)
    col = lambda kk: pl.BlockSpec((None, HGRN_ROWS, d_hgrn), lambda bi, ci, kk=kk: (bi, ci, kk))
    side_specs = []
    for w in side_weights:
        rows, repeat = _cast_blocks(w, b * chunks)
        side_specs.append(pl.BlockSpec((rows, w.shape[1]),
                                       lambda bi, ci, repeat=repeat: ((bi * chunks + ci) // repeat, 0)))
    return pl.pallas_call(
        functools.partial(_hgrn_body, side_casts=len(side_weights)),
        grid=(b, chunks),
        in_specs=[
            col(0), col(1), col(2), col(3),
            _resident(lb_logits.shape),
            _resident((1, HEAD_DIM)),
            _resident(tri.shape),
            _resident(small.shape),
        ] + side_specs,
        out_specs=[pl.BlockSpec((None, HGRN_ROWS, d_hgrn), lambda bi, ci: (bi, ci, 0))] + side_specs,
        out_shape=[jax.ShapeDtypeStruct((b, t, d_hgrn), BF16)]
        + [jax.ShapeDtypeStruct(w.shape, BF16) for w in side_weights],
        scratch_shapes=[pltpu.VMEM((heads, HEAD_DIM, HEAD_DIM), F32)],
        compiler_params=_params(("arbitrary", "arbitrary")),
        name="hgrn",
    )(proj, proj, proj, proj, lb_logits, g_norm, tri, small, *side_weights)


def _mixout_body(yrec_ref, pooled_ref, wpool_ref, pscale_ref, wout_ref, post_g_ref, x_ref, o_ref):
    d_rec = yrec_ref.shape[1]
    group = pooled_ref.shape[1] // len(POOL_WINDOWS)
    mixed = [jnp.dot(pooled_ref[:, gi * group:(gi + 1) * group], wpool_ref[gi],
                     preferred_element_type=F32) for gi in range(len(POOL_WINDOWS))]
    y_pool = (jnp.concatenate(mixed, axis=-1) * pscale_ref[...]).astype(BF16)
    for r in range(x_ref.shape[0] // MIXOUT_ROW_BLOCK):
        start = r * MIXOUT_ROW_BLOCK
        rows = pl.ds(start, MIXOUT_ROW_BLOCK)
        y = jnp.dot(yrec_ref[rows, :], wout_ref[0:d_rec, :], preferred_element_type=F32)
        y = y + jnp.dot(y_pool[start:start + MIXOUT_ROW_BLOCK, :], wout_ref[d_rec:, :],
                        preferred_element_type=F32)
        o_ref[rows, :] = x_ref[rows, :] + y * _rms_scale(y) * post_g_ref[...]


def _mixout(yrec, pooled, w_pool, pool_scale, w_out, post_g, x):
    n, d = x.shape
    row_block = lambda width: pl.BlockSpec((MIXOUT_ROWS, width), lambda i: (i, 0))
    return pl.pallas_call(
        _mixout_body,
        grid=(n // MIXOUT_ROWS,),
        in_specs=[
            row_block(yrec.shape[1]),
            row_block(pooled.shape[1]),
            _resident(w_pool.shape),
            _resident(pool_scale.shape),
            _resident(w_out.shape),
            _resident((1, d)),
            row_block(d),
        ],
        out_specs=row_block(d),
        out_shape=jax.ShapeDtypeStruct((n, d), F32),
        compiler_params=_params(("parallel",)),
        name="mixout",
    )(yrec, pooled, w_pool, pool_scale, w_out, post_g, x)


def kernel(x, ffn1_pre_g, ffn1_w_gate, ffn1_w_up, ffn1_w_down, ffn1_post_g, mix_pre_g, w_in, hgrn_lb_logits, hgrn_g_norm, w_pool, pool_scale, w_out, mix_post_g, ffn2_pre_g, ffn2_w_gate, ffn2_w_up, ffn2_w_down, ffn2_post_g):
    b, t, d = x.shape
    n = b * t
    d_hgrn = hgrn_lb_logits.shape[1]
    depth = w_in.shape[0]
    assert depth == 1 and hgrn_lb_logits.shape[0] == depth + 1
    assert w_in.shape[2] == 4 * d_hgrn + pool_scale.shape[1]
    assert t % HGRN_ROWS == 0 and t % INPROJ_ROWS == 0 and n % MIXOUT_ROWS == 0 and n % FFN_ROWS == 0

    h = x.reshape(n, d)
    h = _ffn(h, ffn1_pre_g, ffn1_w_gate[0].astype(BF16), ffn1_w_up[0].astype(BF16),
             ffn1_w_down[0].astype(BF16), ffn1_post_g)
    proj, pooled = _inproj(h, mix_pre_g, w_in[0].astype(BF16), pool_scale.shape[1], t)
    y_rec, w_out16, wg16, wu16, wd16 = _hgrn(
        proj.reshape(b, t, -1), hgrn_lb_logits, hgrn_g_norm, d_hgrn,
        [w_out[0], ffn2_w_gate[0], ffn2_w_up[0], ffn2_w_down[0]])
    h = _mixout(y_rec.reshape(n, d_hgrn), pooled, w_pool[0].astype(BF16), pool_scale,
                w_out16, mix_post_g, h)
    h = _ffn(h, ffn2_pre_g, wg16, wu16, wd16, ffn2_post_g)
    return h.reshape(b, t, d)
```

```python
import functools
import math

import numpy as np
import jax
import jax.numpy as jnp
from jax import lax
from jax.experimental import pallas as pl
from jax.experimental.pallas import tpu as pltpu

RMS_EPS = 1e-6
HEAD_DIM = 128
SUBLANES = 8
BF16_SUBLANES = 16
POOL_WINDOWS = (2, 4, 8, 16)
POOL_HALO = 16
V7X_VMEM_LIMIT_BYTES = 56 * 1024 * 1024
LOG2E = 1.4426950408889634

FFN_ROWS = 1024
FFN_COLS = 512
FFN_ROW_BLOCK = 512
INPROJ_ROWS = 512
INPROJ_ROW_BLOCK = 256
HGRN_CHUNK = 128
HGRN_ROWS = 512
HGRN_HEAD_GROUP = 4
MIXOUT_ROWS = 1024
MIXOUT_ROW_BLOCK = 256

BF16 = jnp.bfloat16
F32 = jnp.float32


def _rms_scale(v):
    return lax.rsqrt(jnp.mean(v * v, axis=-1, keepdims=True) + RMS_EPS)


def _params(semantics):
    return pltpu.CompilerParams(dimension_semantics=semantics,
                                vmem_limit_bytes=V7X_VMEM_LIMIT_BYTES)


def _resident(shape):
    return pl.BlockSpec(shape, lambda *_: (0,) * len(shape), pipeline_mode=pl.Buffered(1))


def _cast_blocks(w, steps):
    repeat = 1
    while (w.shape[0] * repeat) % steps or (w.shape[0] * repeat // steps) % BF16_SUBLANES:
        repeat *= 2
        assert repeat <= steps, (w.shape, steps)
    return w.shape[0] * repeat // steps, repeat


def _cast_side_inputs(src_refs, dst_refs):
    for src, dst in zip(src_refs, dst_refs):
        dst[...] = src[...].astype(dst.dtype)


def _sigmoid(v):
    return 1.0 / (1.0 + jnp.exp2(v * (-LOG2E)))


def _nt_dot(a, b):
    return lax.dot_general(a, b, (((1,), (1,)), ((), ())), preferred_element_type=F32)


def _ffn_step(x_ref, pre_g_ref, wg_ref, wu_ref, wd_ref, post_g_ref, o_ref, hn_ref, *, first, last):
    for r in range(x_ref.shape[0] // FFN_ROW_BLOCK):
        rows = pl.ds(r * FFN_ROW_BLOCK, FFN_ROW_BLOCK)
        if first:
            x = x_ref[rows, :]
            hn = (x * _rms_scale(x) * pre_g_ref[...]).astype(BF16)
            hn_ref[rows, :] = hn
        else:
            hn = hn_ref[rows, :]
        g = jnp.dot(hn, wg_ref[...], preferred_element_type=F32)
        u = jnp.dot(hn, wu_ref[...], preferred_element_type=F32)
        a = (g * jax.nn.sigmoid(g) * u).astype(BF16)
        acc = jnp.dot(a, wd_ref[...], preferred_element_type=F32)
        if not first:
            acc = o_ref[rows, :] + acc
        if last:
            acc = x_ref[rows, :] + acc * _rms_scale(acc) * (0.5 * post_g_ref[...])
        o_ref[rows, :] = acc


def _ffn_body(*refs):
    j = pl.program_id(1)
    last_j = pl.num_programs(1) - 1
    pl.when(j == 0)(functools.partial(_ffn_step, *refs, first=True, last=False))
    pl.when((j > 0) & (j < last_j))(functools.partial(_ffn_step, *refs, first=False, last=False))
    pl.when(j == last_j)(functools.partial(_ffn_step, *refs, first=False, last=True))


def _ffn(x, pre_g, wg, wu, wd, post_g):
    n, d = x.shape
    dff = wd.shape[0]
    grid = (n // FFN_ROWS, dff // FFN_COLS)
    return pl.pallas_call(
        _ffn_body,
        grid=grid,
        in_specs=[
            pl.BlockSpec((FFN_ROWS, d), lambda i, j: (i, 0)),
            pl.BlockSpec((1, d), lambda i, j: (0, 0)),
            pl.BlockSpec((d, FFN_COLS), lambda i, j: (0, j)),
            pl.BlockSpec((d, FFN_COLS), lambda i, j: (0, j)),
            pl.BlockSpec((FFN_COLS, d), lambda i, j: (j, 0)),
            pl.BlockSpec((1, d), lambda i, j: (0, 0)),
        ],
        out_specs=pl.BlockSpec((FFN_ROWS, d), lambda i, j: (i, 0)),
        out_shape=jax.ShapeDtypeStruct((n, d), F32),
        scratch_shapes=[pltpu.VMEM((FFN_ROWS, d), BF16)],
        compiler_params=_params(("parallel", "arbitrary")),
        name="ffn",
    )(x, pre_g, wg, wu, wd, post_g)


def _pool(u, halo, first_pos):
    rows = u.shape[0]
    group = u.shape[1] // len(POOL_WINDOWS)
    ext = jnp.concatenate([halo, u], axis=0)
    pos = first_pos + lax.broadcasted_iota(jnp.int32, (rows, 1), 0)
    pooled = []
    for gi, w in enumerate(POOL_WINDOWS):
        seg = ext[:, gi * group:(gi + 1) * group]
        acc = seg
        span = 1
        while span < w:
            acc = acc + pltpu.roll(acc, span, 0)
            span *= 2
        count = jnp.minimum(pos + 1, w).astype(F32)
        pooled.append(acc[POOL_HALO:, :] / count - seg[POOL_HALO:, :])
    return jnp.concatenate(pooled, axis=-1)


def _inproj_body(x_ref, g_ref, w_ref, o_ref, pooled_ref, halo_ref, *, tiles_per_seq):
    tile = pl.program_id(0) % tiles_per_seq
    d_rec = o_ref.shape[1]

    @pl.when(tile == 0)
    def _():
        halo_ref[...] = jnp.zeros_like(halo_ref)

    halo = halo_ref[...]
    for r in range(x_ref.shape[0] // INPROJ_ROW_BLOCK):
        start = r * INPROJ_ROW_BLOCK
        rows = pl.ds(start, INPROJ_ROW_BLOCK)
        x = x_ref[rows, :]
        hn = (x * _rms_scale(x) * g_ref[...]).astype(BF16)
        o_ref[rows, :] = jnp.dot(hn, w_ref[:, :d_rec], preferred_element_type=F32)
        u = jnp.dot(hn, w_ref[:, d_rec:], preferred_element_type=F32)
        pooled_ref[rows, :] = _pool(u, halo, tile * x_ref.shape[0] + start).astype(pooled_ref.dtype)
        halo = u[INPROJ_ROW_BLOCK - POOL_HALO:, :]
    halo_ref[...] = halo


def _inproj(x, gain, w, d_pool, tokens_per_seq):
    n, d = x.shape
    d_rec = w.shape[1] - d_pool
    return pl.pallas_call(
        functools.partial(_inproj_body, tiles_per_seq=tokens_per_seq // INPROJ_ROWS),
        grid=(n // INPROJ_ROWS,),
        in_specs=[
            pl.BlockSpec((INPROJ_ROWS, d), lambda i: (i, 0)),
            _resident((1, d)),
            _resident(w.shape),
        ],
        out_specs=[pl.BlockSpec((INPROJ_ROWS, d_rec), lambda i: (i, 0)),
                   pl.BlockSpec((INPROJ_ROWS, d_pool), lambda i: (i, 0))],
        out_shape=[jax.ShapeDtypeStruct((n, d_rec), F32),
                   jax.ShapeDtypeStruct((n, d_pool), BF16)],
        scratch_shapes=[pltpu.VMEM((POOL_HALO, d_pool), F32)],
        compiler_params=_params(("arbitrary",)),
        name="inproj",
    )(x, gain, w)


def _hgrn_constants(chunk):
    t = np.arange(chunk)[:, None]
    s = np.arange(chunk)[None, :]
    tri = (s <= t).astype(np.float32)
    small = []
    for lh in range(1, int(math.log2(SUBLANES))):
        same = (t >> (lh + 1)) == (s >> (lh + 1))
        small.append(same & (((t >> lh) & 1) == 1) & (((s >> lh) & 1) == 0))
    return tri, np.stack(small).astype(np.float32)


def _assemble_scores(near, far, lane):
    n = near.shape[0]
    score_rows = []
    for t0 in range(0, n, SUBLANES):
        acc = None
        h = n // 2
        while h >= SUBLANES:
            if t0 & h:
                bs = t0 & ~(2 * h - 1)
                at = (bs // (2 * h)) * h + (t0 - bs - h)
                piece = far[h][at:at + SUBLANES, :]
                acc = piece if acc is None else jnp.where(lane >= bs, piece, acc)
            h //= 2
        own = near[t0:t0 + SUBLANES, :]
        score_rows.append(own if acc is None else jnp.where(lane >= t0, own, acc))
    return jnp.concatenate(score_rows, axis=0)


def _hgrn_body(*refs, side_casts):
    q_ref, f_ref, i_ref, g_ref, lbl_ref, gn_ref, tri_ref, small_ref = refs[:8]
    y_ref = refs[8 + side_casts]
    state_ref = refs[-1]
    _cast_side_inputs(refs[8:8 + side_casts], refs[9 + side_casts:-1])

    @pl.when(pl.program_id(1) == 0)
    def _():
        state_ref[...] = jnp.zeros_like(state_ref)

    logits = lbl_ref[...]
    e = jnp.exp(logits - jnp.max(logits, axis=0, keepdims=True))
    lb = e[0:1, :] / jnp.sum(e, axis=0, keepdims=True)
    tri = tri_ref[...]
    gn = gn_ref[...]

    width = HGRN_HEAD_GROUP * HEAD_DIM
    for start in range(0, q_ref.shape[0], HGRN_CHUNK):
        rows = pl.ds(start, HGRN_CHUNK)
        for first_head in range(0, q_ref.shape[1] // HEAD_DIM, HGRN_HEAD_GROUP):
            span = slice(first_head * HEAD_DIM, first_head * HEAD_DIM + width)
            y_ref[rows, span] = _hgrn_heads(
                q_ref[rows, span], f_ref[rows, span], i_ref[rows, span], g_ref[rows, span],
                lb[:, span], gn, tri, small_ref, state_ref, first_head).astype(y_ref.dtype)


def _hgrn_heads(q, f_logit, v, gate, lb, gn, tri, small_ref, state_ref, first_head):
    n, width = q.shape
    groups = n // SUBLANES
    head_cols = [slice(c, c + HEAD_DIM) for c in range(0, width, HEAD_DIM)]

    f = lb + (1.0 - lb) * _sigmoid(f_logit)
    k = 1.0 - f
    lg = jnp.log(f) * LOG2E

    g1 = lg.astype(BF16)
    r1 = lg - g1.astype(F32)
    g2 = r1.astype(BF16)
    g3 = (r1 - g2.astype(F32)).astype(BF16)
    cum = jnp.dot(jnp.concatenate([tri, tri, tri], axis=1), jnp.concatenate([g1, g2, g3], axis=0),
                  preferred_element_type=F32)

    sub = lax.broadcasted_iota(jnp.int32, (groups, SUBLANES, width), 1)
    c3 = cum.reshape(groups, SUBLANES, width)
    q3 = q.reshape(groups, SUBLANES, width)
    k3 = k.reshape(groups, SUBLANES, width)

    def row_of(i):
        return jnp.broadcast_to(c3[:, i:i + 1, :], c3.shape)

    near_rows = {}
    for lh, mid in ((1, jnp.where(sub < 4, row_of(1), row_of(5))), (2, row_of(3))):
        upper = (sub & (1 << lh)) != 0
        decay = jnp.exp2((c3 - mid) * jnp.where(upper, 1.0, -1.0))
        near_rows[lh] = (jnp.where(upper, q3, k3) * decay).reshape(n, width).astype(BF16)

    odd = (sub & 1) == 1
    prev_k = pltpu.roll(k3, 1, 1).reshape(n, width)
    prev_v = jnp.where(odd, pltpu.roll(v.reshape(groups, SUBLANES, width), 1, 1), 0.0).reshape(n, width)
    pair = q * f * prev_k
    diag = q * k

    far_q, far_k = {}, {}
    h = SUBLANES
    while h < n:
        starts = range(0, n, 2 * h)
        args, rows = [], []
        for bs in starts:
            mid = cum[bs + h - 1:bs + h, :]
            args += [mid - cum[bs:bs + h, :], cum[bs + h:bs + 2 * h, :] - mid]
            rows += [k[bs:bs + h, :], q[bs + h:bs + 2 * h, :]]
        scaled = jnp.concatenate(rows, axis=0) * jnp.exp2(jnp.concatenate(args, axis=0))
        zeros = jnp.zeros((h, width), F32)
        far_q[h] = jnp.concatenate([scaled[bs + h:bs + 2 * h, :] for bs in starts], axis=0).astype(BF16)
        far_k[h] = jnp.concatenate([blk for bs in starts for blk in (scaled[bs:bs + h, :], zeros)],
                                   axis=0).astype(BF16)
        h *= 2

    last = cum[n - 1:n, :]
    q_in = (q * jnp.exp2(cum)).astype(BF16)
    k_out = (k * jnp.exp2(last - cum)).astype(BF16)
    carry = jnp.exp2(last)
    v16 = v.astype(BF16)

    lane = lax.broadcasted_iota(jnp.int32, (SUBLANES, n), 1)
    outs = []
    for hd, cols in enumerate(head_cols):
        near = None
        for lh, rows16 in near_rows.items():
            part = _nt_dot(rows16[:, cols], rows16[:, cols]) * small_ref[lh - 1]
            near = part if near is None else near + part
        far = {h: _nt_dot(far_q[h][:, cols], far_k[h][:, cols]) for h in far_q}
        scores = _assemble_scores(near, far, lane)

        state_t = state_ref[first_head + hd]
        o = jnp.dot(scores.astype(BF16), v16[:, cols], preferred_element_type=F32)
        o = o + jnp.sum(diag[:, cols], axis=-1, keepdims=True) * v[:, cols]
        o = o + jnp.sum(pair[:, cols], axis=-1, keepdims=True) * prev_v[:, cols]
        o = o + _nt_dot(q_in[:, cols], state_t.astype(BF16))
        state_ref[first_head + hd] = state_t * carry[:, cols] + lax.dot_general(
            v16[:, cols], k_out[:, cols], (((0,), (0,)), ((), ())), preferred_element_type=F32)
        outs.append(o * _rms_scale(o) * gn)

    return jnp.concatenate(outs, axis=-1) * (gate * _sigmoid(gate))


def _hgrn(proj, lb_logits, g_norm, d_hgrn, side_weights):
    b, t, _ = proj.shape
    heads = d_hgrn // HEAD_DIM
    chunks = t // HGRN_ROWS
    tri, small = _hgrn_constants(HGRN_CHUNK)
    tri = jnp.asarray(tri, dtype=BF16)
    small = jnp.asarray(small)
    col = lambda kk: pl.BlockSpec((None, HGRN_ROWS, d_hgrn), lambda bi, ci, kk=kk: (bi, ci, kk))
    side_specs = []
    for w in side_weights:
        rows, repeat = _cast_blocks(w, b * chunks)
        side_specs.append(pl.BlockSpec((rows, w.shape[1]),
                                       lambda bi, ci, repeat=repeat: ((bi * chunks + ci) // repeat, 0)))
    return pl.pallas_call(
        functools.partial(_hgrn_body, side_casts=len(side_weights)),
        grid=(b, chunks),
        in_specs=[
            col(0), col(1), col(2), col(3),
            _resident(lb_logits.shape),
            _resident((1, HEAD_DIM)),
            _resident(tri.shape),
            _resident(small.shape),
        ] + side_specs,
        out_specs=[pl.BlockSpec((None, HGRN_ROWS, d_hgrn), lambda bi, ci: (bi, ci, 0))] + side_specs,
        out_shape=[jax.ShapeDtypeStruct((b, t, d_hgrn), BF16)]
        + [jax.ShapeDtypeStruct(w.shape, BF16) for w in side_weights],
        scratch_shapes=[pltpu.VMEM((heads, HEAD_DIM, HEAD_DIM), F32)],
        compiler_params=_params(("arbitrary", "arbitrary")),
        name="hgrn",
    )(proj, proj, proj, proj, lb_logits, g_norm, tri, small, *side_weights)


def _mixout_body(yrec_ref, pooled_ref, wpool_ref, pscale_ref, wout_ref, post_g_ref, x_ref, o_ref):
    d_rec = yrec_ref.shape[1]
    group = pooled_ref.shape[1] // len(POOL_WINDOWS)
    mixed = [jnp.dot(pooled_ref[:, gi * group:(gi + 1) * group], wpool_ref[gi],
                     preferred_element_type=F32) for gi in range(len(POOL_WINDOWS))]
    y_pool = (jnp.concatenate(mixed, axis=-1) * pscale_ref[...]).astype(BF16)
    for r in range(x_ref.shape[0] // MIXOUT_ROW_BLOCK):
        start = r * MIXOUT_ROW_BLOCK
        rows = pl.ds(start, MIXOUT_ROW_BLOCK)
        y = jnp.dot(yrec_ref[rows, :], wout_ref[0:d_rec, :], preferred_element_type=F32)
        y = y + jnp.dot(y_pool[start:start + MIXOUT_ROW_BLOCK, :], wout_ref[d_rec:, :],
                        preferred_element_type=F32)
        o_ref[rows, :] = x_ref[rows, :] + y * _rms_scale(y) * post_g_ref[...]


def _mixout(yrec, pooled, w_pool, pool_scale, w_out, post_g, x):
    n, d = x.shape
    row_block = lambda width: pl.BlockSpec((MIXOUT_ROWS, width), lambda i: (i, 0))
    return pl.pallas_call(
        _mixout_body,
        grid=(n // MIXOUT_ROWS,),
        in_specs=[
            row_block(yrec.shape[1]),
            row_block(pooled.shape[1]),
            _resident(w_pool.shape),
            _resident(pool_scale.shape),
            _resident(w_out.shape),
            _resident((1, d)),
            row_block(d),
        ],
        out_specs=row_block(d),
        out_shape=jax.ShapeDtypeStruct((n, d), F32),
        compiler_params=_params(("parallel",)),
        name="mixout",
    )(yrec, pooled, w_pool, pool_scale, w_out, post_g, x)


def kernel(x, ffn1_pre_g, ffn1_w_gate, ffn1_w_up, ffn1_w_down, ffn1_post_g, mix_pre_g, w_in, hgrn_lb_logits, hgrn_g_norm, w_pool, pool_scale, w_out, mix_post_g, ffn2_pre_g, ffn2_w_gate, ffn2_w_up, ffn2_w_down, ffn2_post_g):
    b, t, d = x.shape
    n = b * t
    d_hgrn = hgrn_lb_logits.shape[1]
    depth = w_in.shape[0]
    assert depth == 1 and hgrn_lb_logits.shape[0] == depth + 1
    assert w_in.shape[2] == 4 * d_hgrn + pool_scale.shape[1]
    assert t % HGRN_ROWS == 0 and t % INPROJ_ROWS == 0 and n % MIXOUT_ROWS == 0 and n % FFN_ROWS == 0

    h = x.reshape(n, d)
    h = _ffn(h, ffn1_pre_g, ffn1_w_gate[0].astype(BF16), ffn1_w_up[0].astype(BF16),
             ffn1_w_down[0].astype(BF16), ffn1_post_g)
    proj, pooled = _inproj(h, mix_pre_g, w_in[0].astype(BF16), pool_scale.shape[1], t)
    y_rec, w_out16, wg16, wu16, wd16 = _hgrn(
        proj.reshape(b, t, -1), hgrn_lb_logits, hgrn_g_norm, d_hgrn,
        [w_out[0], ffn2_w_gate[0], ffn2_w_up[0], ffn2_w_down[0]])
    h = _mixout(y_rec.reshape(n, d_hgrn), pooled, w_pool[0].astype(BF16), pool_scale,
                w_out16, mix_post_g, h)
    h = _ffn(h, ffn2_pre_g, wg16, wu16, wd16, ffn2_post_g)
    return h.reshape(b, t, d)
```

```python
import functools
import math

import numpy as np
import jax
import jax.numpy as jnp
from jax import lax
from jax.experimental import pallas as pl
from jax.experimental.pallas import tpu as pltpu

RMS_EPS = 1e-6
HEAD_DIM = 128
SUBLANES = 8
BF16_SUBLANES = 16
POOL_WINDOWS = (2, 4, 8, 16)
POOL_HALO = 16
V7X_VMEM_LIMIT_BYTES = 56 * 1024 * 1024
LOG2E = 1.4426950408889634

FFN_ROWS = 1024
FFN_COLS = 512
FFN_ROW_BLOCK = 512
INPROJ_ROWS = 512
INPROJ_ROW_BLOCK = 256
HGRN_CHUNK = 128
HGRN_ROWS = 512
HGRN_HEAD_GROUP = 8
MIXOUT_ROWS = 1024
MIXOUT_ROW_BLOCK = 256

BF16 = jnp.bfloat16
F32 = jnp.float32


def _rms_scale(v):
    return lax.rsqrt(jnp.mean(v * v, axis=-1, keepdims=True) + RMS_EPS)


def _params(semantics):
    return pltpu.CompilerParams(dimension_semantics=semantics,
                                vmem_limit_bytes=V7X_VMEM_LIMIT_BYTES)


def _resident(shape):
    return pl.BlockSpec(shape, lambda *_: (0,) * len(shape), pipeline_mode=pl.Buffered(1))


def _cast_blocks(w, steps):
    repeat = 1
    while (w.shape[0] * repeat) % steps or (w.shape[0] * repeat // steps) % BF16_SUBLANES:
        repeat *= 2
        assert repeat <= steps, (w.shape, steps)
    return w.shape[0] * repeat // steps, repeat


def _cast_side_inputs(src_refs, dst_refs):
    for src, dst in zip(src_refs, dst_refs):
        dst[...] = src[...].astype(dst.dtype)


def _sigmoid(v):
    return 1.0 / (1.0 + jnp.exp2(v * (-LOG2E)))


def _nt_dot(a, b):
    return lax.dot_general(a, b, (((1,), (1,)), ((), ())), preferred_element_type=F32)


def _ffn_step(x_ref, pre_g_ref, wg_ref, wu_ref, wd_ref, post_g_ref, o_ref, hn_ref, *, first, last):
    for r in range(x_ref.shape[0] // FFN_ROW_BLOCK):
        rows = pl.ds(r * FFN_ROW_BLOCK, FFN_ROW_BLOCK)
        if first:
            x = x_ref[rows, :]
            hn = (x * _rms_scale(x) * pre_g_ref[...]).astype(BF16)
            hn_ref[rows, :] = hn
        else:
            hn = hn_ref[rows, :]
        g = jnp.dot(hn, wg_ref[...], preferred_element_type=F32)
        u = jnp.dot(hn, wu_ref[...], preferred_element_type=F32)
        a = (g * jax.nn.sigmoid(g) * u).astype(BF16)
        acc = jnp.dot(a, wd_ref[...], preferred_element_type=F32)
        if not first:
            acc = o_ref[rows, :] + acc
        if last:
            acc = x_ref[rows, :] + acc * _rms_scale(acc) * (0.5 * post_g_ref[...])
        o_ref[rows, :] = acc


def _ffn_body(*refs):
    j = pl.program_id(1)
    last_j = pl.num_programs(1) - 1
    pl.when(j == 0)(functools.partial(_ffn_step, *refs, first=True, last=False))
    pl.when((j > 0) & (j < last_j))(functools.partial(_ffn_step, *refs, first=False, last=False))
    pl.when(j == last_j)(functools.partial(_ffn_step, *refs, first=False, last=True))


def _ffn(x, pre_g, wg, wu, wd, post_g):
    n, d = x.shape
    dff = wd.shape[0]
    grid = (n // FFN_ROWS, dff // FFN_COLS)
    return pl.pallas_call(
        _ffn_body,
        grid=grid,
        in_specs=[
            pl.BlockSpec((FFN_ROWS, d), lambda i, j: (i, 0)),
            pl.BlockSpec((1, d), lambda i, j: (0, 0)),
            pl.BlockSpec((d, FFN_COLS), lambda i, j: (0, j)),
            pl.BlockSpec((d, FFN_COLS), lambda i, j: (0, j)),
            pl.BlockSpec((FFN_COLS, d), lambda i, j: (j, 0)),
            pl.BlockSpec((1, d), lambda i, j: (0, 0)),
        ],
        out_specs=pl.BlockSpec((FFN_ROWS, d), lambda i, j: (i, 0)),
        out_shape=jax.ShapeDtypeStruct((n, d), F32),
        scratch_shapes=[pltpu.VMEM((FFN_ROWS, d), BF16)],
        compiler_params=_params(("parallel", "arbitrary")),
        name="ffn",
    )(x, pre_g, wg, wu, wd, post_g)


def _pool(u, halo, first_pos):
    rows = u.shape[0]
    group = u.shape[1] // len(POOL_WINDOWS)
    ext = jnp.concatenate([halo, u], axis=0)
    pos = first_pos + lax.broadcasted_iota(jnp.int32, (rows, 1), 0)
    pooled = []
    for gi, w in enumerate(POOL_WINDOWS):
        seg = ext[:, gi * group:(gi + 1) * group]
        acc = seg
        span = 1
        while span < w:
            acc = acc + pltpu.roll(acc, span, 0)
            span *= 2
        count = jnp.minimum(pos + 1, w).astype(F32)
        pooled.append(acc[POOL_HALO:, :] / count - seg[POOL_HALO:, :])
    return jnp.concatenate(pooled, axis=-1)


def _inproj_body(x_ref, g_ref, w_ref, o_ref, pooled_ref, halo_ref, *, tiles_per_seq):
    tile = pl.program_id(0) % tiles_per_seq
    d_rec = o_ref.shape[1]

    @pl.when(tile == 0)
    def _():
        halo_ref[...] = jnp.zeros_like(halo_ref)

    halo = halo_ref[...]
    for r in range(x_ref.shape[0] // INPROJ_ROW_BLOCK):
        start = r * INPROJ_ROW_BLOCK
        rows = pl.ds(start, INPROJ_ROW_BLOCK)
        x = x_ref[rows, :]
        hn = (x * _rms_scale(x) * g_ref[...]).astype(BF16)
        o_ref[rows, :] = jnp.dot(hn, w_ref[:, :d_rec], preferred_element_type=F32)
        u = jnp.dot(hn, w_ref[:, d_rec:], preferred_element_type=F32)
        pooled_ref[rows, :] = _pool(u, halo, tile * x_ref.shape[0] + start).astype(pooled_ref.dtype)
        halo = u[INPROJ_ROW_BLOCK - POOL_HALO:, :]
    halo_ref[...] = halo


def _inproj(x, gain, w, d_pool, tokens_per_seq):
    n, d = x.shape
    d_rec = w.shape[1] - d_pool
    return pl.pallas_call(
        functools.partial(_inproj_body, tiles_per_seq=tokens_per_seq // INPROJ_ROWS),
        grid=(n // INPROJ_ROWS,),
        in_specs=[
            pl.BlockSpec((INPROJ_ROWS, d), lambda i: (i, 0)),
            _resident((1, d)),
            _resident(w.shape),
        ],
        out_specs=[pl.BlockSpec((INPROJ_ROWS, d_rec), lambda i: (i, 0)),
                   pl.BlockSpec((INPROJ_ROWS, d_pool), lambda i: (i, 0))],
        out_shape=[jax.ShapeDtypeStruct((n, d_rec), F32),
                   jax.ShapeDtypeStruct((n, d_pool), BF16)],
        scratch_shapes=[pltpu.VMEM((POOL_HALO, d_pool), F32)],
        compiler_params=_params(("arbitrary",)),
        name="inproj",
    )(x, gain, w)


def _hgrn_constants(chunk):
    t = np.arange(chunk)[:, None]
    s = np.arange(chunk)[None, :]
    tri = (s <= t).astype(np.float32)
    small = []
    for lh in range(1, int(math.log2(SUBLANES))):
        same = (t >> (lh + 1)) == (s >> (lh + 1))
        small.append(same & (((t >> lh) & 1) == 1) & (((s >> lh) & 1) == 0))
    return tri, np.stack(small).astype(np.float32)


def _assemble_scores(near, far, lane):
    n = near.shape[0]
    score_rows = []
    for t0 in range(0, n, SUBLANES):
        acc = None
        h = n // 2
        while h >= SUBLANES:
            if t0 & h:
                bs = t0 & ~(2 * h - 1)
                at = (bs // (2 * h)) * h + (t0 - bs - h)
                piece = far[h][at:at + SUBLANES, :]
                acc = piece if acc is None else jnp.where(lane >= bs, piece, acc)
            h //= 2
        own = near[t0:t0 + SUBLANES, :]
        score_rows.append(own if acc is None else jnp.where(lane >= t0, own, acc))
    return jnp.concatenate(score_rows, axis=0)


def _hgrn_body(*refs, side_casts):
    q_ref, f_ref, i_ref, g_ref, lbl_ref, gn_ref, tri_ref, small_ref = refs[:8]
    y_ref = refs[8 + side_casts]
    state_ref = refs[-1]
    _cast_side_inputs(refs[8:8 + side_casts], refs[9 + side_casts:-1])

    @pl.when(pl.program_id(1) == 0)
    def _():
        state_ref[...] = jnp.zeros_like(state_ref)

    logits = lbl_ref[...]
    e = jnp.exp(logits - jnp.max(logits, axis=0, keepdims=True))
    lb = e[0:1, :] / jnp.sum(e, axis=0, keepdims=True)
    tri = tri_ref[...]
    gn = gn_ref[...]

    width = HGRN_HEAD_GROUP * HEAD_DIM
    for start in range(0, q_ref.shape[0], HGRN_CHUNK):
        rows = pl.ds(start, HGRN_CHUNK)
        for first_head in range(0, q_ref.shape[1] // HEAD_DIM, HGRN_HEAD_GROUP):
            span = slice(first_head * HEAD_DIM, first_head * HEAD_DIM + width)
            y_ref[rows, span] = _hgrn_heads(
                q_ref[rows, span], f_ref[rows, span], i_ref[rows, span], g_ref[rows, span],
                lb[:, span], gn, tri, small_ref, state_ref, first_head).astype(y_ref.dtype)


def _hgrn_heads(q, f_logit, v, gate, lb, gn, tri, small_ref, state_ref, first_head):
    n, width = q.shape
    groups = n // SUBLANES
    head_cols = [slice(c, c + HEAD_DIM) for c in range(0, width, HEAD_DIM)]

    f = lb + (1.0 - lb) * _sigmoid(f_logit)
    k = 1.0 - f
    lg = jnp.log(f) * LOG2E

    g1 = lg.astype(BF16)
    r1 = lg - g1.astype(F32)
    g2 = r1.astype(BF16)
    g3 = (r1 - g2.astype(F32)).astype(BF16)
    cum = jnp.dot(jnp.concatenate([tri, tri, tri], axis=1), jnp.concatenate([g1, g2, g3], axis=0),
                  preferred_element_type=F32)

    sub = lax.broadcasted_iota(jnp.int32, (groups, SUBLANES, width), 1)
    c3 = cum.reshape(groups, SUBLANES, width)
    q3 = q.reshape(groups, SUBLANES, width)
    k3 = k.reshape(groups, SUBLANES, width)

    def row_of(i):
        return jnp.broadcast_to(c3[:, i:i + 1, :], c3.shape)

    near_rows = {}
    for lh, mid in ((1, jnp.where(sub < 4, row_of(1), row_of(5))), (2, row_of(3))):
        upper = (sub & (1 << lh)) != 0
        decay = jnp.exp2((c3 - mid) * jnp.where(upper, 1.0, -1.0))
        near_rows[lh] = (jnp.where(upper, q3, k3) * decay).reshape(n, width).astype(BF16)

    odd = (sub & 1) == 1
    prev_k = pltpu.roll(k3, 1, 1).reshape(n, width)
    prev_v = jnp.where(odd, pltpu.roll(v.reshape(groups, SUBLANES, width), 1, 1), 0.0).reshape(n, width)
    pair = q * f * prev_k
    diag = q * k

    far_q, far_k = {}, {}
    h = SUBLANES
    while h < n:
        starts = range(0, n, 2 * h)
        args, rows = [], []
        for bs in starts:
            mid = cum[bs + h - 1:bs + h, :]
            args += [mid - cum[bs:bs + h, :], cum[bs + h:bs + 2 * h, :] - mid]
            rows += [k[bs:bs + h, :], q[bs + h:bs + 2 * h, :]]
        scaled = jnp.concatenate(rows, axis=0) * jnp.exp2(jnp.concatenate(args, axis=0))
        zeros = jnp.zeros((h, width), F32)
        far_q[h] = jnp.concatenate([scaled[bs + h:bs + 2 * h, :] for bs in starts], axis=0).astype(BF16)
        far_k[h] = jnp.concatenate([blk for bs in starts for blk in (scaled[bs:bs + h, :], zeros)],
                                   axis=0).astype(BF16)
        h *= 2

    last = cum[n - 1:n, :]
    q_in = (q * jnp.exp2(cum)).astype(BF16)
    k_out = (k * jnp.exp2(last - cum)).astype(BF16)
    carry = jnp.exp2(last)
    v16 = v.astype(BF16)

    lane = lax.broadcasted_iota(jnp.int32, (SUBLANES, n), 1)
    outs = []
    for hd, cols in enumerate(head_cols):
        near = None
        for lh, rows16 in near_rows.items():
            part = _nt_dot(rows16[:, cols], rows16[:, cols]) * small_ref[lh - 1]
            near = part if near is None else near + part
        far = {h: _nt_dot(far_q[h][:, cols], far_k[h][:, cols]) for h in far_q}
        scores = _assemble_scores(near, far, lane)

        state_t = state_ref[first_head + hd]
        v_t = jnp.transpose(v16[:, cols])
        o = _nt_dot(jnp.concatenate([scores.astype(BF16), q_in[:, cols]], axis=1),
                    jnp.concatenate([v_t, state_t.astype(BF16)], axis=1))
        o = o + jnp.sum(diag[:, cols], axis=-1, keepdims=True) * v[:, cols]
        o = o + jnp.sum(pair[:, cols], axis=-1, keepdims=True) * prev_v[:, cols]
        state_ref[first_head + hd] = state_t * carry[:, cols] + jnp.dot(
            v_t, k_out[:, cols], preferred_element_type=F32)
        outs.append(o * _rms_scale(o) * gn)

    return jnp.concatenate(outs, axis=-1) * (gate * _sigmoid(gate))


def _hgrn(proj, lb_logits, g_norm, d_hgrn, side_weights):
    b, t, _ = proj.shape
    heads = d_hgrn // HEAD_DIM
    chunks = t // HGRN_ROWS
    tri, small = _hgrn_constants(HGRN_CHUNK)
    tri = jnp.asarray(tri, dtype=BF16)
    small = jnp.asarray(small)
    col = lambda kk: pl.BlockSpec((None, HGRN_ROWS, d_hgrn), lambda bi, ci, kk=kk: (bi, ci, kk))
    side_specs = []
    for w in side_weights:
        rows, repeat = _cast_blocks(w, b * chunks)
        side_specs.append(pl.BlockSpec((rows, w.shape[1]),
                                       lambda bi, ci, repeat=repeat: ((bi * chunks + ci) // repeat, 0)))
    return pl.pallas_call(
        functools.partial(_hgrn_body, side_casts=len(side_weights)),
        grid=(b, chunks),
        in_specs=[
            col(0), col(1), col(2), col(3),
            _resident(lb_logits.shape),
            _resident((1, HEAD_DIM)),
            _resident(tri.shape),
            _resident(small.shape),
        ] + side_specs,
        out_specs=[pl.BlockSpec((None, HGRN_ROWS, d_hgrn), lambda bi, ci: (bi, ci, 0))] + side_specs,
        out_shape=[jax.ShapeDtypeStruct((b, t, d_hgrn), BF16)]
        + [jax.ShapeDtypeStruct(w.shape, BF16) for w in side_weights],
        scratch_shapes=[pltpu.VMEM((heads, HEAD_DIM, HEAD_DIM), F32)],
        compiler_params=_params(("arbitrary", "arbitrary")),
        name="hgrn",
    )(proj, proj, proj, proj, lb_logits, g_norm, tri, small, *side_weights)


def _mixout_body(yrec_ref, pooled_ref, wpool_ref, pscale_ref, wout_ref, post_g_ref, x_ref, o_ref):
    d_rec = yrec_ref.shape[1]
    group = pooled_ref.shape[1] // len(POOL_WINDOWS)
    mixed = [jnp.dot(pooled_ref[:, gi * group:(gi + 1) * group], wpool_ref[gi],
                     preferred_element_type=F32) for gi in range(len(POOL_WINDOWS))]
    y_pool = (jnp.concatenate(mixed, axis=-1) * pscale_ref[...]).astype(BF16)
    for r in range(x_ref.shape[0] // MIXOUT_ROW_BLOCK):
        start = r * MIXOUT_ROW_BLOCK
        rows = pl.ds(start, MIXOUT_ROW_BLOCK)
        y = jnp.dot(yrec_ref[rows, :], wout_ref[0:d_rec, :], preferred_element_type=F32)
        y = y + jnp.dot(y_pool[start:start + MIXOUT_ROW_BLOCK, :], wout_ref[d_rec:, :],
                        preferred_element_type=F32)
        o_ref[rows, :] = x_ref[rows, :] + y * _rms_scale(y) * post_g_ref[...]


def _mixout(yrec, pooled, w_pool, pool_scale, w_out, post_g, x):
    n, d = x.shape
    row_block = lambda width: pl.BlockSpec((MIXOUT_ROWS, width), lambda i: (i, 0))
    return pl.pallas_call(
        _mixout_body,
        grid=(n // MIXOUT_ROWS,),
        in_specs=[
            row_block(yrec.shape[1]),
            row_block(pooled.shape[1]),
            _resident(w_pool.shape),
            _resident(pool_scale.shape),
            _resident(w_out.shape),
            _resident((1, d)),
            row_block(d),
        ],
        out_specs=row_block(d),
        out_shape=jax.ShapeDtypeStruct((n, d), F32),
        compiler_params=_params(("parallel",)),
        name="mixout",
    )(yrec, pooled, w_pool, pool_scale, w_out, post_g, x)


def kernel(x, ffn1_pre_g, ffn1_w_gate, ffn1_w_up, ffn1_w_down, ffn1_post_g, mix_pre_g, w_in, hgrn_lb_logits, hgrn_g_norm, w_pool, pool_scale, w_out, mix_post_g, ffn2_pre_g, ffn2_w_gate, ffn2_w_up, ffn2_w_down, ffn2_post_g):
    b, t, d = x.shape
    n = b * t
    d_hgrn = hgrn_lb_logits.shape[1]
    depth = w_in.shape[0]
    assert depth == 1 and hgrn_lb_logits.shape[0] == depth + 1
    assert w_in.shape[2] == 4 * d_hgrn + pool_scale.shape[1]
    assert t % HGRN_ROWS == 0 and t % INPROJ_ROWS == 0 and n % MIXOUT_ROWS == 0 and n % FFN_ROWS == 0

    h = x.reshape(n, d)
    h = _ffn(h, ffn1_pre_g, ffn1_w_gate[0].astype(BF16), ffn1_w_up[0].astype(BF16),
             ffn1_w_down[0].astype(BF16), ffn1_post_g)
    proj, pooled = _inproj(h, mix_pre_g, w_in[0].astype(BF16), pool_scale.shape[1], t)
    y_rec, w_out16, wg16, wu16, wd16 = _hgrn(
        proj.reshape(b, t, -1), hgrn_lb_logits, hgrn_g_norm, d_hgrn,
        [w_out[0], ffn2_w_gate[0], ffn2_w_up[0], ffn2_w_down[0]])
    h = _mixout(y_rec.reshape(n, d_hgrn), pooled, w_pool[0].astype(BF16), pool_scale,
                w_out16, mix_post_g, h)
    h = _ffn(h, ffn2_pre_g, wg16, wu16, wd16, ffn2_post_g)
    return h.reshape(b, t, d)
```
